```python
import jax, jax.numpy as jnp
from jax import lax
import numpy as np

D_MODEL = 2048
BATCH = 8
SEQ = 2048
DEPTH = 2

MOBA_HEADS = 8
MOBA_HEAD_DIM = 128
MOBA_BLOCK = 256
MOBA_TOPK = 3
MOBA_Q_CHUNK = 64
RET_HEADS = 4
RET_KEY_DIM = 256
RET_VALUE_DIM = 512
RET_CHUNK = 128
ROPE_BASE = 10000.0
MEM_LEN = 256
MEM_HEADS = 4
MEM_HEAD_DIM = 256
N_BRANCHES = 3
MOBA_WIDTH = MOBA_HEADS * MOBA_HEAD_DIM
RET_QK_WIDTH = RET_HEADS * RET_KEY_DIM
RET_V_WIDTH = RET_HEADS * RET_VALUE_DIM
MEM_WIDTH = MEM_HEADS * MEM_HEAD_DIM
IN_WIDTH = 3 * MOBA_WIDTH + 2 * RET_QK_WIDTH + 2 * RET_V_WIDTH + MEM_WIDTH + N_BRANCHES * D_MODEL
N_GROUPS = 4
EXPERTS_PER_GROUP = 8
N_EXPERTS = N_GROUPS * EXPERTS_PER_GROUP
EXPERT_TOPK = 2
EXPERT_FF = 512
EXPERT_BLOCK = 128
DEEPNORM_ALPHA = (2 * DEPTH) ** 0.25
DEEPNORM_BETA = (8 * DEPTH) ** -0.25
LN_EPS = 1e-5
GN_EPS = 1e-6
NEG_INF = -1e30

kernel_name = "hybrid_moba_retention_memory_hmoe_deepnorm"

F32 = jnp.float32


def layer_norm(x, g, b):
    xf = x.astype(F32)
    mu = xf.mean(-1, keepdims=True)
    var = jnp.mean(jnp.square(xf - mu), -1, keepdims=True)
    return ((xf - mu) * lax.rsqrt(var + LN_EPS) * g.astype(F32) + b.astype(F32)).astype(x.dtype)


def rotary(x, pos):
    half = x.shape[-1] // 2
    inv = ROPE_BASE ** (-jnp.linspace(0.0, 1.0, half, dtype=F32))
    ang = pos.astype(F32)[:, None] * inv[None, :]
    cos = jnp.cos(ang)[None, :, None, :]
    sin = jnp.sin(ang)[None, :, None, :]
    x1 = x[..., :half].astype(F32)
    x2 = x[..., half:].astype(F32)
    return jnp.concatenate([x1 * cos - x2 * sin, x1 * sin + x2 * cos], -1).astype(x.dtype)


def moba_attention(q, k, v):
    bsz, seq, nh, dh = q.shape
    n_blk = -(-seq // MOBA_BLOCK)
    s_pad = n_blk * MOBA_BLOCK
    top_n = min(MOBA_TOPK, n_blk)
    padw = ((0, 0), (0, s_pad - seq), (0, 0), (0, 0))
    qh = jnp.pad(q, padw).transpose(0, 2, 1, 3)
    kb = jnp.pad(k, padw).transpose(0, 2, 1, 3).reshape(bsz, nh, n_blk, MOBA_BLOCK, dh)
    vb = jnp.pad(v, padw).transpose(0, 2, 1, 3).reshape(bsz, nh, n_blk, MOBA_BLOCK, dh)
    k_mean = kb.astype(F32).mean(axis=3)
    q_blk = jnp.arange(s_pad) // MOBA_BLOCK
    past = jnp.arange(n_blk)[None, :] < q_blk[:, None]
    gate = jnp.einsum('bhsd,bhnd->bhsn', qh.astype(F32), k_mean)
    gate = jnp.where(past, gate, NEG_INF)
    _, sel = lax.top_k(gate, top_n)
    valid = sel < q_blk[:, None]
    scale = dh ** -0.5
    n_qc = s_pad // MOBA_Q_CHUNK

    def chunk_major(t):
        t = t.reshape(bsz, nh, n_qc, MOBA_Q_CHUNK, *t.shape[3:])
        return jnp.moveaxis(t, 2, 1)

    qc, selc, validc = chunk_major(qh), chunk_major(sel), chunk_major(valid)

    def one_batch(args):
        q_b, sel_b, valid_b, k_b, v_b = args

        def one_chunk(cargs):
            c, q_c, sel_c, valid_c = cargs
            k_sel = jax.vmap(lambda kh, sh: kh[sh])(k_b, sel_c)
            v_sel = jax.vmap(lambda vh, sh: vh[sh])(v_b, sel_c)
            s_sel = jnp.einsum('hqd,hqnkd->hqnk', q_c, k_sel, preferred_element_type=F32) * scale
            s_sel = jnp.where(valid_c[..., None], s_sel, NEG_INF)
            s_sel = s_sel.reshape(nh, MOBA_Q_CHUNK, top_n * MOBA_BLOCK)
            q_pos = c * MOBA_Q_CHUNK + jnp.arange(MOBA_Q_CHUNK)
            own = (c * MOBA_Q_CHUNK) // MOBA_BLOCK
            k_own = lax.dynamic_index_in_dim(k_b, own, axis=1, keepdims=False)
            v_own = lax.dynamic_index_in_dim(v_b, own, axis=1, keepdims=False)
            k_pos = own * MOBA_BLOCK + jnp.arange(MOBA_BLOCK)
            s_own = jnp.einsum('hqd,hkd->hqk', q_c, k_own, preferred_element_type=F32) * scale
            s_own = jnp.where(k_pos[None, :] <= q_pos[:, None], s_own, NEG_INF)
            p = jax.nn.softmax(jnp.concatenate([s_sel, s_own], -1), axis=-1)
            p_sel = p[..., :top_n * MOBA_BLOCK].reshape(nh, MOBA_Q_CHUNK, top_n, MOBA_BLOCK).astype(v_b.dtype)
            p_own = p[..., top_n * MOBA_BLOCK:].astype(v_b.dtype)
            return (jnp.einsum('hqnk,hqnkd->hqd', p_sel, v_sel)
                    + jnp.einsum('hqk,hkd->hqd', p_own, v_own))

        return lax.map(one_chunk, (jnp.arange(n_qc), q_b, sel_b, valid_b))

    out = lax.map(one_batch, (qc, selc, validc, kb, vb))
    out = jnp.moveaxis(out, 2, 1).reshape(bsz, nh, s_pad, dh)[:, :, :seq]
    return out.transpose(0, 2, 1, 3).reshape(bsz, seq, nh * dh)


def retention(q, k, v, g):
    bsz, seq, nh, dk = q.shape
    dv = v.shape[-1]
    pos = jnp.arange(seq)
    q = rotary(q, pos)
    k = rotary(k, pos) * (dk ** -0.5)
    n_c = seq // RET_CHUNK
    log_g = jnp.log1p(-jnp.power(2.0, -5.0 - jnp.arange(nh, dtype=F32)))
    idx = jnp.arange(RET_CHUNK, dtype=F32)
    diff = idx[:, None] - idx[None, :]
    decay = jnp.where(diff >= 0, jnp.exp(log_g[:, None, None] * jnp.maximum(diff, 0.0)), 0.0)

    def to_chunks(t):
        return t.reshape(bsz, n_c, RET_CHUNK, nh, t.shape[-1]).transpose(0, 3, 1, 2, 4).astype(F32)

    qc, kc, vc = to_chunks(q), to_chunks(k), to_chunks(v)
    scores = jnp.einsum('bhnid,bhnjd->bhnij', qc, kc) * decay[None, :, None]
    inner = jnp.einsum('bhnij,bhnje->bhnie', scores, vc)
    zeta = jnp.exp(log_g[:, None] * (RET_CHUNK - 1 - idx)[None, :])
    kv = jnp.einsum('bhnjd,bhnje->nbhde', kc * zeta[None, :, None, :, None], vc)
    chunk_decay = jnp.exp(log_g * RET_CHUNK)[None, :, None, None]

    def step(state, kv_n):
        return chunk_decay * state + kv_n, state

    _, states = lax.scan(step, jnp.zeros((bsz, nh, dk, dv), F32), kv)
    xi = jnp.exp(log_g[:, None] * (idx + 1.0)[None, :])
    cross = jnp.einsum('bhnid,nbhde->bhnie', qc * xi[None, :, None, :, None], states)
    o = (inner + cross).transpose(0, 2, 3, 1, 4).reshape(bsz, seq, nh, dv)
    mu = o.mean(-1, keepdims=True)
    var = jnp.mean(jnp.square(o - mu), -1, keepdims=True)
    o = ((o - mu) * lax.rsqrt(var + GN_EPS)).reshape(bsz, seq, nh * dv)
    return (jax.nn.silu(g.astype(F32)) * o).astype(v.dtype)


def memory_attention(qm, mem, w_mem_kv):
    bsz, seq, _ = qm.shape
    kv = mem @ w_mem_kv
    km, vm = jnp.split(kv, 2, axis=-1)
    q = qm.reshape(bsz, seq, MEM_HEADS, MEM_HEAD_DIM)
    km = km.reshape(bsz, -1, MEM_HEADS, MEM_HEAD_DIM)
    vm = vm.reshape(bsz, -1, MEM_HEADS, MEM_HEAD_DIM)
    s = jnp.einsum('bshd,bmhd->bhsm', q, km, preferred_element_type=F32) * (MEM_HEAD_DIM ** -0.5)
    p = jax.nn.softmax(s, axis=-1).astype(vm.dtype)
    return jnp.einsum('bhsm,bmhd->bshd', p, vm).reshape(bsz, seq, MEM_WIDTH)


def hybrid_mixer(u, mem, w_in, p_moba, p_ret, p_mem, w_mem_kv, w_o):
    bsz, seq, _ = u.shape
    proj = u @ w_in
    sizes = ((MOBA_WIDTH,) * 3 + (RET_QK_WIDTH,) * 2 + (RET_V_WIDTH,) * 2
             + (MEM_WIDTH,) + (D_MODEL,) * N_BRANCHES)
    cuts = [int(c) for c in np.cumsum(sizes)[:-1]]
    aq, ak, av, rq, rk, rv, rg, mq, ga, gr, gm = jnp.split(proj, cuts, axis=-1)

    def heads(t, h):
        return t.reshape(bsz, seq, h, -1)

    y_a = moba_attention(heads(aq, MOBA_HEADS), heads(ak, MOBA_HEADS), heads(av, MOBA_HEADS))
    y_r = retention(heads(rq, RET_HEADS), heads(rk, RET_HEADS), heads(rv, RET_HEADS), rg)
    y_m = memory_attention(mq, mem, w_mem_kv)
    merged = (jax.nn.sigmoid(ga) * (y_a @ p_moba)
              + jax.nn.sigmoid(gr) * (y_r @ p_ret)
              + jax.nn.sigmoid(gm) * (y_m @ p_mem))
    return merged @ w_o


def hier_moe(x, w_group, b_group, w_expert, b_expert, w_gate_up, w_down):
    bsz, seq, d = x.shape
    n_tok = bsz * seq
    xt = x.reshape(n_tok, d)
    g_logits = (xt @ w_group).astype(F32) + b_group.astype(F32)
    g_prob = jax.nn.softmax(g_logits, axis=-1)
    _, g_sel = lax.top_k(g_logits, 1)
    p_g = jnp.take_along_axis(g_prob, g_sel, axis=1)
    e_logits = ((xt @ w_expert).astype(F32) + b_expert.astype(F32)).reshape(n_tok, N_GROUPS, EXPERTS_PER_GROUP)
    e_in = jnp.take_along_axis(e_logits, g_sel[:, :, None], axis=1)[:, 0]
    top_v, top_i = lax.top_k(e_in, EXPERT_TOPK)
    weights = p_g * jax.nn.softmax(top_v, axis=-1)
    expert_id = g_sel * EXPERTS_PER_GROUP + top_i

    n_asg = n_tok * EXPERT_TOPK
    flat_e = expert_id.reshape(n_asg)
    flat_t = jnp.repeat(jnp.arange(n_tok, dtype=jnp.int32), EXPERT_TOPK)
    flat_w = weights.reshape(n_asg)
    order = jnp.argsort(flat_e)
    e_s, t_s, w_s = flat_e[order], flat_t[order], flat_w[order]
    counts = jnp.bincount(flat_e, length=N_EXPERTS)
    starts = jnp.cumsum(counts) - counts
    padded = ((counts + EXPERT_BLOCK - 1) // EXPERT_BLOCK) * EXPERT_BLOCK
    pad_ends = jnp.cumsum(padded)
    pad_starts = pad_ends - padded
    dest = pad_starts[e_s] + (jnp.arange(n_asg) - starts[e_s])
    n_pad = (-(-n_asg // EXPERT_BLOCK) + N_EXPERTS) * EXPERT_BLOCK
    row_tok = jnp.zeros((n_pad,), jnp.int32).at[dest].set(t_s)
    row_w = jnp.zeros((n_pad,), F32).at[dest].set(w_s)
    n_blocks = n_pad // EXPERT_BLOCK
    block_e = jnp.clip(jnp.searchsorted(pad_ends, jnp.arange(n_blocks) * EXPERT_BLOCK, side='right'),
                       0, N_EXPERTS - 1)
    x_rows = xt[row_tok].reshape(n_blocks, EXPERT_BLOCK, d)

    def run_block(args):
        xb, e = args
        gate, up = jnp.split(xb @ w_gate_up[e], 2, axis=-1)
        return (jax.nn.silu(gate) * up) @ w_down[e]

    y_rows = lax.map(run_block, (x_rows, block_e)).reshape(n_pad, d)
    out = jax.ops.segment_sum(y_rows * row_w[:, None].astype(y_rows.dtype), row_tok, num_segments=n_tok)
    return out.reshape(bsz, seq, d)


def setup_inputs(seed: int = 0) -> dict:
    key = jax.random.key(seed)
    ks = jax.random.split(key, 18)

    def nrm(k, shape, scale):
        return jax.random.normal(k, shape, F32) * scale

    L = DEPTH
    return {
        "x": nrm(ks[0], (BATCH, SEQ, D_MODEL), 1.0),
        "mem": nrm(ks[1], (BATCH, MEM_LEN, D_MODEL), 1.0),
        "w_in": nrm(ks[2], (L, D_MODEL, IN_WIDTH), D_MODEL ** -0.5),
        "p_moba": nrm(ks[3], (L, MOBA_WIDTH, D_MODEL), MOBA_WIDTH ** -0.5),
        "p_ret": nrm(ks[4], (L, RET_V_WIDTH, D_MODEL), RET_V_WIDTH ** -0.5),
        "p_mem": nrm(ks[5], (L, MEM_WIDTH, D_MODEL), MEM_WIDTH ** -0.5),
        "w_mem_kv": nrm(ks[6], (L, D_MODEL, 2 * MEM_WIDTH), D_MODEL ** -0.5),
        "w_o": nrm(ks[7], (L, D_MODEL, D_MODEL), D_MODEL ** -0.5 * DEEPNORM_BETA),
        "ln1_g": 1.0 + nrm(ks[8], (L, D_MODEL), 0.02),
        "ln1_b": nrm(ks[9], (L, D_MODEL), 0.02),
        "w_group": nrm(ks[10], (L, D_MODEL, N_GROUPS), D_MODEL ** -0.5),
        "b_group": nrm(ks[11], (L, N_GROUPS), 0.01),
        "w_expert": nrm(ks[12], (L, D_MODEL, N_EXPERTS), D_MODEL ** -0.5),
        "b_expert": nrm(ks[13], (L, N_EXPERTS), 0.01),
        "w_gate_up": nrm(ks[14], (L, N_EXPERTS, D_MODEL, 2 * EXPERT_FF), D_MODEL ** -0.5),
        "w_down": nrm(ks[15], (L, N_EXPERTS, EXPERT_FF, D_MODEL), EXPERT_FF ** -0.5 * DEEPNORM_BETA),
        "ln2_g": 1.0 + nrm(ks[16], (L, D_MODEL), 0.02),
        "ln2_b": nrm(ks[17], (L, D_MODEL), 0.02),
    }


def reference(x, mem, w_in, p_moba, p_ret, p_mem, w_mem_kv, w_o, ln1_g, ln1_b,
              w_group, b_group, w_expert, b_expert, w_gate_up, w_down, ln2_g, ln2_b):
    for l in range(DEPTH):
        mix = hybrid_mixer(x, mem, w_in[l], p_moba[l], p_ret[l], p_mem[l], w_mem_kv[l], w_o[l])
        x = layer_norm(DEEPNORM_ALPHA * x + mix, ln1_g[l], ln1_b[l])
        ffn = hier_moe(x, w_group[l], b_group[l], w_expert[l], b_expert[l], w_gate_up[l], w_down[l])
        x = layer_norm(DEEPNORM_ALPHA * x + ffn, ln2_g[l], ln2_b[l])
    return x
```

```python
import functools

import jax
import jax.numpy as jnp
from jax import lax
from jax.experimental import pallas as pl
from jax.experimental.pallas import tpu as pltpu

F32 = jnp.float32
BF16 = jnp.bfloat16

D_MODEL = 2048
DEPTH = 2
MOBA_HEADS = 8
MOBA_HEAD_DIM = 128
MOBA_BLOCK = 256
MOBA_TOPK = 3
RET_HEADS = 4
RET_KEY_DIM = 256
RET_VALUE_DIM = 512
RET_CHUNK = 128
ROPE_BASE = 10000.0
MEM_HEADS = 4
MEM_HEAD_DIM = 256
N_BRANCHES = 3
MOBA_WIDTH = MOBA_HEADS * MOBA_HEAD_DIM
RET_QK_WIDTH = RET_HEADS * RET_KEY_DIM
RET_V_WIDTH = RET_HEADS * RET_VALUE_DIM
MEM_WIDTH = MEM_HEADS * MEM_HEAD_DIM
N_GROUPS = 4
EXPERTS_PER_GROUP = 8
N_EXPERTS = N_GROUPS * EXPERTS_PER_GROUP
EXPERT_TOPK = 2
EXPERT_FF = 512
DEEPNORM_ALPHA = (2 * DEPTH) ** 0.25
LN_EPS = 1e-5
GN_EPS = 1e-6
NEG_INF = -1e30

OFF_AQ = 0
OFF_AK = OFF_AQ + MOBA_WIDTH
OFF_AV = OFF_AK + MOBA_WIDTH
OFF_RQ = OFF_AV + MOBA_WIDTH
OFF_RK = OFF_RQ + RET_QK_WIDTH
OFF_RV = OFF_RK + RET_QK_WIDTH
OFF_RG = OFF_RV + RET_V_WIDTH
OFF_MQ = OFF_RG + RET_V_WIDTH
OFF_GA = OFF_MQ + MEM_WIDTH
OFF_GR = OFF_GA + D_MODEL
OFF_GM = OFF_GR + D_MODEL

LANES = 128
SUBLANES = 8
VMEM_LIMIT = 56 * 1024 * 1024

EXPERT_ROWS = 256
ROUTER_ROWS = 48
ROW_TILE = 256

_NT = (((1,), (1,)), ((), ()))


def _params(*sem):
    return pltpu.CompilerParams(dimension_semantics=sem, vmem_limit_bytes=VMEM_LIMIT)


def _mm_kernel(a_ref, b_ref, o_ref):
    o_ref[...] = jnp.dot(a_ref[...], b_ref[...], preferred_element_type=F32).astype(o_ref.dtype)


def _matmul(a, b, out_dtype, tm, tn):
    m, k = a.shape
    _, n = b.shape
    tm, tn = min(tm, m), min(tn, n)
    return pl.pallas_call(
        _mm_kernel,
        grid=(n // tn, m // tm),
        in_specs=[pl.BlockSpec((tm, k), lambda j, i: (i, 0)),
                  pl.BlockSpec((k, tn), lambda j, i: (0, j))],
        out_specs=pl.BlockSpec((tm, tn), lambda j, i: (i, j)),
        out_shape=jax.ShapeDtypeStruct((m, n), out_dtype),
        compiler_params=_params("parallel", "parallel"),
        name="matmul",
    )(a, b)


def _moba_kernel(q_ref, k_ref, v_ref, o_ref, km_ref, vt_ref, sel_ref, *, n_blk, blk, top_n, scale):
    i = pl.program_id(2)
    seq = n_blk * blk
    dh = q_ref.shape[1]
    shift = blk.bit_length() - 1

    @pl.when(i == 0)
    def _():
        row = lax.broadcasted_iota(jnp.int32, (2 * SUBLANES, seq), 0)
        col_blk = lax.shift_right_logical(lax.broadcasted_iota(jnp.int32, (2 * SUBLANES, seq), 1), shift)
        ind = jnp.where(col_blk == (row & (SUBLANES - 1)), 1.0, 0.0).astype(BF16)
        kmean = jnp.dot(ind, k_ref[...], preferred_element_type=F32) * (1.0 / blk)
        hi = kmean.astype(BF16).astype(F32)
        r2 = lax.broadcasted_iota(jnp.int32, (2 * SUBLANES, dh), 0)
        km_ref[...] = jnp.where(r2 < SUBLANES, hi, kmean - hi).astype(BF16)
        for j in range(n_blk):
            vt_ref[j] = v_ref[j * blk:(j + 1) * blk, :].astype(F32).T.astype(BF16)

    q = q_ref[...]
    g16 = lax.dot_general(km_ref[...], q, _NT, preferred_element_type=F32)
    g = g16[0:SUBLANES] + g16[SUBLANES:2 * SUBLANES]
    jrow = lax.broadcasted_iota(jnp.int32, (SUBLANES, blk), 0)
    past = jrow < i
    g = jnp.where(past, g, NEG_INF)
    rank = jnp.zeros((SUBLANES, blk), F32)
    for jp in range(SUBLANES):
        gj = g[jp:jp + 1, :]
        beats = jnp.where(gj > g, 1.0, jnp.where((gj == g) & (jp < jrow), 1.0, 0.0))
        rank = rank + beats
    sel_ref[...] = jnp.where(past & (rank < top_n), 1.0, 0.0)

    k_own = k_ref[pl.ds(pl.multiple_of(i * blk, blk), blk), :]
    s = lax.dot_general(k_own, q, _NT, preferred_element_type=F32) * scale
    kpos = lax.broadcasted_iota(jnp.int32, (blk, blk), 0)
    qpos = lax.broadcasted_iota(jnp.int32, (blk, blk), 1)
    s = jnp.where(kpos <= qpos, s, NEG_INF)
    m0 = jnp.max(s, axis=0, keepdims=True)
    p = jnp.exp(s - m0)
    l0 = jnp.sum(p, axis=0, keepdims=True)
    acc0 = jnp.dot(vt_ref[i], p.astype(BF16), preferred_element_type=F32)

    def body(j, carry):
        m, l, acc = carry
        kj = k_ref[pl.ds(pl.multiple_of(j * blk, blk), blk), :]
        sj = lax.dot_general(kj, q, _NT, preferred_element_type=F32) * scale
        sj = jnp.where(sel_ref[pl.ds(j, 1), :] > 0.5, sj, NEG_INF)
        m_new = jnp.maximum(m, jnp.max(sj, axis=0, keepdims=True))
        alpha = jnp.exp(m - m_new)
        pj = jnp.exp(sj - m_new)
        l = alpha * l + jnp.sum(pj, axis=0, keepdims=True)
        acc = alpha * acc + jnp.dot(vt_ref[j], pj.astype(BF16), preferred_element_type=F32)
        return m_new, l, acc

    _, l, acc = lax.fori_loop(0, i, body, (m0, l0, acc0))
    o_ref[...] = (acc * (1.0 / l)).T.astype(o_ref.dtype)


def _moba(proj, bsz, seq):
    n_blk = seq // MOBA_BLOCK
    assert seq % MOBA_BLOCK == 0 and n_blk <= SUBLANES
    dh = MOBA_HEAD_DIM
    kern = functools.partial(_moba_kernel, n_blk=n_blk, blk=MOBA_BLOCK, top_n=min(MOBA_TOPK, n_blk),
                             scale=dh ** -0.5)
    return pl.pallas_call(
        kern,
        grid=(bsz, MOBA_HEADS, n_blk),
        in_specs=[pl.BlockSpec((MOBA_BLOCK, dh), lambda b, h, i: (b * n_blk + i, OFF_AQ // dh + h)),
                  pl.BlockSpec((seq, dh), lambda b, h, i: (b, OFF_AK // dh + h)),
                  pl.BlockSpec((seq, dh), lambda b, h, i: (b, OFF_AV // dh + h))],
        out_specs=pl.BlockSpec((MOBA_BLOCK, dh), lambda b, h, i: (b * n_blk + i, h)),
        out_shape=jax.ShapeDtypeStruct((bsz * seq, MOBA_WIDTH), BF16),
        scratch_shapes=[pltpu.VMEM((2 * SUBLANES, dh), BF16),
                        pltpu.VMEM((n_blk, dh, MOBA_BLOCK), BF16),
                        pltpu.VMEM((SUBLANES, MOBA_BLOCK), F32)],
        compiler_params=_params("parallel", "parallel", "arbitrary"),
        name="moba",
    )(proj, proj, proj)


def _ret_kernel(cd_ref, q_ref, k_ref, v_ref, g_ref, cos_ref, sin_ref, dec_ref, zeta_ref, xi_ref,
                o_ref, st_ref, *, dk):
    h = pl.program_id(1)
    n = pl.program_id(2)

    @pl.when(n == 0)
    def _():
        st_ref[...] = jnp.zeros_like(st_ref)

    half = dk // 2
    cos = cos_ref[...]
    sin = sin_ref[...]

    def rot(x):
        x1 = x[:, :half]
        x2 = x[:, half:]
        return jnp.concatenate([x1 * cos - x2 * sin, x1 * sin + x2 * cos], axis=1)

    def widen(t):
        return jnp.concatenate([t] * (dk // LANES), axis=1)

    q = rot(q_ref[...].astype(F32))
    k = rot(k_ref[...].astype(F32)) * (dk ** -0.5)
    v = v_ref[...]
    sc = lax.dot_general(q.astype(BF16), k.astype(BF16), _NT, preferred_element_type=F32) * dec_ref[0]
    inner = jnp.dot(sc.astype(BF16), v, preferred_element_type=F32)
    st = st_ref[...]
    cross = jnp.dot((q * widen(xi_ref[0])).astype(BF16), st.astype(BF16), preferred_element_type=F32)
    o = inner + cross
    kz = k * widen(zeta_ref[0])
    kv = jnp.dot(kz.T.astype(BF16), v, preferred_element_type=F32)
    st_ref[...] = cd_ref[h] * st + kv

    mu = jnp.mean(o, axis=1, keepdims=True)
    d = o - mu
    var = jnp.mean(d * d, axis=1, keepdims=True)
    gg = g_ref[...].astype(F32)
    o_ref[...] = (gg * jax.nn.sigmoid(gg) * (d * lax.rsqrt(var + GN_EPS))).astype(o_ref.dtype)


def _retention(proj, bsz, seq):
    c, dk, dv, nh = RET_CHUNK, RET_KEY_DIM, RET_VALUE_DIM, RET_HEADS
    n_c = seq // c
    half = dk // 2
    pos = jnp.arange(seq, dtype=F32)
    inv = ROPE_BASE ** (-jnp.linspace(0.0, 1.0, half, dtype=F32))
    ang = pos[:, None] * inv[None, :]
    cos, sin = jnp.cos(ang), jnp.sin(ang)
    log_g = jnp.log1p(-jnp.power(2.0, -5.0 - jnp.arange(nh, dtype=F32)))
    idx = jnp.arange(c, dtype=F32)
    diff = idx[:, None] - idx[None, :]
    decay = jnp.where(diff >= 0, jnp.exp(log_g[:, None, None] * jnp.maximum(diff, 0.0)), 0.0)
    zeta = jnp.exp(log_g[:, None] * (c - 1 - idx)[None, :])
    xi = jnp.exp(log_g[:, None] * (idx + 1.0)[None, :])
    zeta_b = jnp.broadcast_to(zeta[:, :, None], (nh, c, LANES))
    xi_b = jnp.broadcast_to(xi[:, :, None], (nh, c, LANES))
    chunk_decay = jnp.exp(log_g * c)

    return pl.pallas_call(
        functools.partial(_ret_kernel, dk=dk),
        grid=(bsz, nh, n_c),
        in_specs=[pl.BlockSpec(memory_space=pltpu.SMEM),
                  pl.BlockSpec((c, dk), lambda b, h, n: (b * n_c + n, OFF_RQ // dk + h)),
                  pl.BlockSpec((c, dk), lambda b, h, n: (b * n_c + n, OFF_RK // dk + h)),
                  pl.BlockSpec((c, dv), lambda b, h, n: (b * n_c + n, OFF_RV // dv + h)),
                  pl.BlockSpec((c, dv), lambda b, h, n: (b * n_c + n, OFF_RG // dv + h)),
                  pl.BlockSpec((c, half), lambda b, h, n: (n, 0)),
                  pl.BlockSpec((c, half), lambda b, h, n: (n, 0)),
                  pl.BlockSpec((1, c, c), lambda b, h, n: (h, 0, 0)),
                  pl.BlockSpec((1, c, LANES), lambda b, h, n: (h, 0, 0)),
                  pl.BlockSpec((1, c, LANES), lambda b, h, n: (h, 0, 0))],
        out_specs=pl.BlockSpec((c, dv), lambda b, h, n: (b * n_c + n, h)),
        out_shape=jax.ShapeDtypeStruct((bsz * seq, RET_V_WIDTH), BF16),
        scratch_shapes=[pltpu.VMEM((dk, dv), F32)],
        compiler_params=_params("parallel", "parallel", "arbitrary"),
        name="retention",
    )(chunk_decay, proj, proj, proj, proj, cos, sin, decay, zeta_b, xi_b)


def _mem_kernel(q_ref, k_ref, v_ref, o_ref, *, scale):
    s = lax.dot_general(q_ref[...], k_ref[...], _NT, preferred_element_type=F32) * scale
    m = jnp.max(s, axis=1, keepdims=True)
    p = jnp.exp(s - m)
    p = p * (1.0 / jnp.sum(p, axis=1, keepdims=True))
    o_ref[...] = jnp.dot(p.astype(BF16), v_ref[...], preferred_element_type=F32).astype(o_ref.dtype)


def _mem_attention(proj, kvm, bsz, seq, mem_len):
    dh, nh = MEM_HEAD_DIM, MEM_HEADS
    tq = min(512, seq)
    n_q = seq // tq
    return pl.pallas_call(
        functools.partial(_mem_kernel, scale=dh ** -0.5),
        grid=(bsz, nh, n_q),
        in_specs=[pl.BlockSpec((tq, dh), lambda b, h, i: (b * n_q + i, OFF_MQ // dh + h)),
                  pl.BlockSpec((mem_len, dh), lambda b, h, i: (b, h)),
                  pl.BlockSpec((mem_len, dh), lambda b, h, i: (b, nh + h))],
        out_specs=pl.BlockSpec((tq, dh), lambda b, h, i: (b * n_q + i, h)),
        out_shape=jax.ShapeDtypeStruct((bsz * seq, MEM_WIDTH), BF16),
        compiler_params=_params("parallel", "parallel", "parallel"),
        name="mem_attention",
    )(proj, kvm, kvm)


def _merge_kernel(ya_ref, yr_ref, ym_ref, pa_ref, pr_ref, pm_ref, ga_ref, gr_ref, gm_ref, o_ref):
    def branch(y_ref, p_ref, gate_ref):
        t = jnp.dot(y_ref[...], p_ref[...], preferred_element_type=F32)
        return jax.nn.sigmoid(gate_ref[...].astype(F32)) * t

    o_ref[...] = (branch(ya_ref, pa_ref, ga_ref) + branch(yr_ref, pr_ref, gr_ref)
                  + branch(ym_ref, pm_ref, gm_ref)).astype(o_ref.dtype)


def _merge(y_a, y_r, y_m, p_moba, p_ret, p_mem, proj):
    m = y_a.shape[0]
    tm, tn = min(512, m), 512
    d = D_MODEL

    def rows(width):
        return pl.BlockSpec((tm, width), lambda j, i: (i, 0))

    def wcols(kdim):
        return pl.BlockSpec((kdim, tn), lambda j, i: (0, j))

    def gate(off):
        return pl.BlockSpec((tm, tn), lambda j, i: (i, off // tn + j))

    return pl.pallas_call(
        _merge_kernel,
        grid=(d // tn, m // tm),
        in_specs=[rows(MOBA_WIDTH), rows(RET_V_WIDTH), rows(MEM_WIDTH),
                  wcols(MOBA_WIDTH), wcols(RET_V_WIDTH), wcols(MEM_WIDTH),
                  gate(OFF_GA), gate(OFF_GR), gate(OFF_GM)],
        out_specs=pl.BlockSpec((tm, tn), lambda j, i: (i, j)),
        out_shape=jax.ShapeDtypeStruct((m, d), BF16),
        compiler_params=_params("parallel", "parallel"),
        name="merge",
    )(y_a, y_r, y_m, p_moba, p_ret, p_mem, proj, proj, proj)


def _layer_norm_rows(z, g, b):
    mu = jnp.mean(z, axis=1, keepdims=True)
    d = z - mu
    var = jnp.mean(d * d, axis=1, keepdims=True)
    return d * lax.rsqrt(var + LN_EPS) * g + b


def _wo_ln_router_kernel(mg_ref, wo_ref, x_ref, g_ref, b_ref, wr2_ref, wr1_ref, br_ref,
                         x1_ref, ids_ref, wts_ref):
    mix = jnp.dot(mg_ref[...], wo_ref[...], preferred_element_type=F32)
    x1 = _layer_norm_rows(DEEPNORM_ALPHA * x_ref[...] + mix, g_ref[...], b_ref[...])
    x1_ref[...] = x1

    r = ROUTER_ROWS
    x_hi = x1.astype(BF16)
    x_lo = (x1 - x_hi.astype(F32)).astype(BF16)
    l2 = lax.dot_general(wr2_ref[...], x_hi, _NT, preferred_element_type=F32)
    l1 = lax.dot_general(wr1_ref[...], x_lo, _NT, preferred_element_type=F32)
    logit = l2[0:r] + l2[r:2 * r] + l1 + br_ref[...]

    tok = logit.shape[1]
    row = lax.broadcasted_iota(jnp.int32, (SUBLANES, tok), 0).astype(F32)
    gl = jnp.where(row < N_GROUPS, logit[0:SUBLANES], NEG_INF)
    g_max = jnp.max(gl, axis=0, keepdims=True)
    g_sel = jnp.min(jnp.where(gl == g_max, row, float(SUBLANES)), axis=0, keepdims=True)
    p_g = 1.0 / jnp.sum(jnp.exp(gl - g_max), axis=0, keepdims=True)
    e_in = jnp.zeros((SUBLANES, tok), F32)
    for grp in range(N_GROUPS):
        lo = SUBLANES * (1 + grp)
        e_in = jnp.where(g_sel == float(grp), logit[lo:lo + EXPERTS_PER_GROUP], e_in)
    v1 = jnp.max(e_in, axis=0, keepdims=True)
    i1 = jnp.min(jnp.where(e_in == v1, row, float(SUBLANES)), axis=0, keepdims=True)
    e_rest = jnp.where(row == i1, -jnp.inf, e_in)
    v2 = jnp.max(e_rest, axis=0, keepdims=True)
    i2 = jnp.min(jnp.where(e_rest == v2, row, float(SUBLANES)), axis=0, keepdims=True)
    t = jnp.exp(v2 - v1)
    w1 = p_g / (1.0 + t)
    w2 = w1 * t
    e1 = g_sel * float(EXPERTS_PER_GROUP) + i1
    e2 = g_sel * float(EXPERTS_PER_GROUP) + i2
    ids_ref[...] = jnp.where(row == 0.0, e1, jnp.where(row == 1.0, e2, 0.0)).astype(jnp.int32)
    wts_ref[...] = jnp.where(row == 0.0, w1, jnp.where(row == 1.0, w2, 0.0))


def _wo_ln_router(merged, w_o, x, ln_g, ln_b, w_group, b_group, w_expert, b_expert):
    m, d = x.shape
    tm = min(ROW_TILE, m)
    r = ROUTER_ROWS
    wr = jnp.zeros((r, d), F32).at[0:N_GROUPS].set(w_group.T).at[SUBLANES:SUBLANES + N_EXPERTS].set(w_expert.T)
    br = jnp.zeros((r, 1), F32).at[0:N_GROUPS, 0].set(b_group).at[SUBLANES:SUBLANES + N_EXPERTS, 0].set(b_expert)
    wr_hi = wr.astype(BF16)
    wr_lo = (wr - wr_hi.astype(F32)).astype(BF16)
    wr2 = jnp.concatenate([wr_hi, wr_lo], axis=0)

    def whole(shape):
        return pl.BlockSpec(shape, lambda i: (0,) * len(shape))

    return pl.pallas_call(
        _wo_ln_router_kernel,
        grid=(m // tm,),
        in_specs=[pl.BlockSpec((tm, d), lambda i: (i, 0)),
                  whole((d, d)),
                  pl.BlockSpec((tm, d), lambda i: (i, 0)),
                  whole((1, d)), whole((1, d)),
                  whole((2 * r, d)), whole((r, d)), whole((r, 1))],
        out_specs=[pl.BlockSpec((tm, d), lambda i: (i, 0)),
                   pl.BlockSpec((SUBLANES, tm), lambda i: (0, i)),
                   pl.BlockSpec((SUBLANES, tm), lambda i: (0, i))],
        out_shape=[jax.ShapeDtypeStruct((m, d), F32),
                   jax.ShapeDtypeStruct((SUBLANES, m), jnp.int32),
                   jax.ShapeDtypeStruct((SUBLANES, m), F32)],
        compiler_params=_params("parallel"),
        name="wo_ln_router",
    )(merged, w_o, x, ln_g.reshape(1, d), ln_b.reshape(1, d), wr2, wr_hi, br)


def _row_copy(src, src_row, dst, dst_row, sem):
    return pltpu.make_async_copy(src.at[pl.ds(src_row, 1)], dst.at[pl.ds(dst_row, 1)], sem)


def _dispatch_kernel(d0_ref, d1_ref, x_hbm, zero_hbm, o_hbm, sem, *, tb):
    del zero_hbm
    base = pl.program_id(0) * tb

    def issue(t, c):
        tok = base + t
        _row_copy(x_hbm, tok, o_hbm, d0_ref[tok], sem).start()
        _row_copy(x_hbm, tok, o_hbm, d1_ref[tok], sem).start()
        return c

    lax.fori_loop(0, tb, issue, 0)

    def drain(t, c):
        _row_copy(x_hbm, 0, o_hbm, 0, sem).wait()
        _row_copy(x_hbm, 0, o_hbm, 0, sem).wait()
        return c

    lax.fori_loop(0, tb, drain, 0)


def _dispatch(x1, dest0, dest1, n_pad):
    m, d = x1.shape
    tb = min(ROW_TILE, m)
    zeros = jnp.zeros((n_pad, d), F32)
    return pl.pallas_call(
        functools.partial(_dispatch_kernel, tb=tb),
        grid_spec=pltpu.PrefetchScalarGridSpec(
            num_scalar_prefetch=2,
            grid=(m // tb,),
            in_specs=[pl.BlockSpec(memory_space=pl.ANY), pl.BlockSpec(memory_space=pl.ANY)],
            out_specs=pl.BlockSpec(memory_space=pl.ANY),
            scratch_shapes=[pltpu.SemaphoreType.DMA(())]),
        out_shape=jax.ShapeDtypeStruct((n_pad, d), F32),
        input_output_aliases={3: 0},
        compiler_params=_params("arbitrary"),
        name="dispatch",
    )(dest0, dest1, x1, zeros)


def _expert_kernel(be_ref, nu_ref, x_ref, wgu_ref, wd_ref, o_ref):
    del be_ref
    i = pl.program_id(0)

    @pl.when(i < nu_ref[0])
    def _():
        ff = wd_ref.shape[1]
        gu = jnp.dot(x_ref[...].astype(BF16), wgu_ref[0], preferred_element_type=F32)
        gate = gu[:, :ff]
        hmid = gate * jax.nn.sigmoid(gate) * gu[:, ff:]
        o_ref[...] = jnp.dot(hmid.astype(BF16), wd_ref[0], preferred_element_type=F32)

    @pl.when(i >= nu_ref[0])
    def _():
        o_ref[...] = jnp.zeros_like(o_ref)


def _experts(x_rows, block_e, n_used, w_gate_up, w_down):
    n_pad, d = x_rows.shape
    ff = w_down.shape[1]
    rows = EXPERT_ROWS
    return pl.pallas_call(
        _expert_kernel,
        grid_spec=pltpu.PrefetchScalarGridSpec(
            num_scalar_prefetch=2,
            grid=(n_pad // rows,),
            in_specs=[pl.BlockSpec((rows, d), lambda i, be, nu: (i, 0)),
                      pl.BlockSpec((1, d, 2 * ff), lambda i, be, nu: (be[i], 0, 0)),
                      pl.BlockSpec((1, ff, d), lambda i, be, nu: (be[i], 0, 0))],
            out_specs=pl.BlockSpec((rows, d), lambda i, be, nu: (i, 0))),
        out_shape=jax.ShapeDtypeStruct((n_pad, d), F32),
        compiler_params=_params("arbitrary"),
        name="experts",
    )(block_e, n_used, x_rows, w_gate_up, w_down)


def _combine_ln_kernel(d0_ref, d1_ref, y_hbm, w_ref, x_ref, g_ref, b_ref, o_ref, ob_ref, buf, sem, *, tb):
    base = pl.program_id(0) * tb

    def issue(t, c):
        tok = base + t
        _row_copy(y_hbm, d0_ref[tok], buf.at[0], t, sem).start()
        _row_copy(y_hbm, d1_ref[tok], buf.at[1], t, sem).start()
        return c

    lax.fori_loop(0, tb, issue, 0)

    def drain(t, c):
        _row_copy(y_hbm, 0, buf.at[0], 0, sem).wait()
        _row_copy(y_hbm, 0, buf.at[1], 0, sem).wait()
        return c

    lax.fori_loop(0, tb, drain, 0)

    w = w_ref[...]
    ffn = w[:, 0:1] * buf[0] + w[:, 1:2] * buf[1]
    out = _layer_norm_rows(DEEPNORM_ALPHA * x_ref[...] + ffn, g_ref[...], b_ref[...])
    o_ref[...] = out
    ob_ref[...] = out.astype(BF16)


def _combine_ln(y_rows, dest0, dest1, w_tok, x1, ln_g, ln_b):
    m, d = x1.shape
    tb = min(ROW_TILE, m)
    return pl.pallas_call(
        functools.partial(_combine_ln_kernel, tb=tb),
        grid_spec=pltpu.PrefetchScalarGridSpec(
            num_scalar_prefetch=2,
            grid=(m // tb,),
            in_specs=[pl.BlockSpec(memory_space=pl.ANY),
                      pl.BlockSpec((tb, EXPERT_TOPK), lambda i, a, b: (i, 0)),
                      pl.BlockSpec((tb, d), lambda i, a, b: (i, 0)),
                      pl.BlockSpec((1, d), lambda i, a, b: (0, 0)),
                      pl.BlockSpec((1, d), lambda i, a, b: (0, 0))],
            out_specs=[pl.BlockSpec((tb, d), lambda i, a, b: (i, 0)),
                       pl.BlockSpec((tb, d), lambda i, a, b: (i, 0))],
            scratch_shapes=[pltpu.VMEM((EXPERT_TOPK, tb, d), F32), pltpu.SemaphoreType.DMA(())]),
        out_shape=[jax.ShapeDtypeStruct((m, d), F32), jax.ShapeDtypeStruct((m, d), BF16)],
        compiler_params=_params("arbitrary"),
        name="combine_ln",
    )(dest0, dest1, y_rows, w_tok, x1, ln_g.reshape(1, d), ln_b.reshape(1, d))


def _slot_plan(ids, n_tok):
    rows = EXPERT_ROWS
    n_asg = n_tok * EXPERT_TOPK
    n_pad = n_asg + N_EXPERTS * rows
    flat_e = ids[:EXPERT_TOPK].reshape(n_asg)
    onehot = (flat_e[:, None] == jnp.arange(N_EXPERTS, dtype=jnp.int32)[None, :]).astype(jnp.int32)
    csum = jnp.cumsum(onehot, axis=0)
    pos = jnp.sum(onehot * csum, axis=1) - 1
    counts = csum[-1]
    padded = ((counts + rows - 1) // rows) * rows
    pad_ends = jnp.cumsum(padded)
    pad_starts = pad_ends - padded
    dest = (jnp.sum(onehot * pad_starts[None, :], axis=1) + pos).astype(jnp.int32)
    n_blocks = n_pad // rows
    block_e = jnp.clip(jnp.searchsorted(pad_ends, jnp.arange(n_blocks, dtype=jnp.int32) * rows, side='right'),
                       0, N_EXPERTS - 1).astype(jnp.int32)
    n_used = (pad_ends[-1:] // rows).astype(jnp.int32)
    return dest[:n_tok], dest[n_tok:], block_e, n_used, n_pad


def kernel(x, mem, w_in, p_moba, p_ret, p_mem, w_mem_kv, w_o, ln1_g, ln1_b, w_group, b_group,
           w_expert, b_expert, w_gate_up, w_down, ln2_g, ln2_b):
    bsz, seq, d = x.shape
    mem_len = mem.shape[1]
    n_tok = bsz * seq
    xf = x.reshape(n_tok, d)
    xb = xf.astype(BF16)
    mem_b = mem.reshape(bsz * mem_len, d).astype(BF16)
    for l in range(w_in.shape[0]):
        proj = _matmul(xb, w_in[l].astype(BF16), BF16, 512, 1024)
        kvm = _matmul(mem_b, w_mem_kv[l].astype(BF16), BF16, 512, 1024)
        y_a = _moba(proj, bsz, seq)
        y_r = _retention(proj, bsz, seq)
        y_m = _mem_attention(proj, kvm, bsz, seq, mem_len)
        merged = _merge(y_a, y_r, y_m, p_moba[l].astype(BF16), p_ret[l].astype(BF16),
                        p_mem[l].astype(BF16), proj)
        x1, ids, wts = _wo_ln_router(merged, w_o[l].astype(BF16), xf, ln1_g[l], ln1_b[l],
                                     w_group[l], b_group[l], w_expert[l], b_expert[l])
        dest0, dest1, block_e, n_used, n_pad = _slot_plan(ids, n_tok)
        x_rows = _dispatch(x1, dest0, dest1, n_pad)
        y_rows = _experts(x_rows, block_e, n_used, w_gate_up[l].astype(BF16), w_down[l].astype(BF16))
        xf, xb = _combine_ln(y_rows, dest0, dest1, wts[:EXPERT_TOPK].T, x1, ln2_g[l], ln2_b[l])
    return xf.reshape(bsz, seq, d)
```

```python
import functools

import jax
import jax.numpy as jnp
from jax import lax
from jax.experimental import pallas as pl
from jax.experimental.pallas import tpu as pltpu

F32 = jnp.float32
BF16 = jnp.bfloat16

D_MODEL = 2048
DEPTH = 2
MOBA_HEADS = 8
MOBA_HEAD_DIM = 128
MOBA_BLOCK = 256
MOBA_TOPK = 3
RET_HEADS = 4
RET_KEY_DIM = 256
RET_VALUE_DIM = 512
RET_CHUNK = 128
ROPE_BASE = 10000.0
MEM_HEADS = 4
MEM_HEAD_DIM = 256
N_BRANCHES = 3
MOBA_WIDTH = MOBA_HEADS * MOBA_HEAD_DIM
RET_QK_WIDTH = RET_HEADS * RET_KEY_DIM
RET_V_WIDTH = RET_HEADS * RET_VALUE_DIM
MEM_WIDTH = MEM_HEADS * MEM_HEAD_DIM
N_GROUPS = 4
EXPERTS_PER_GROUP = 8
N_EXPERTS = N_GROUPS * EXPERTS_PER_GROUP
EXPERT_TOPK = 2
EXPERT_FF = 512
DEEPNORM_ALPHA = (2 * DEPTH) ** 0.25
LN_EPS = 1e-5
GN_EPS = 1e-6
NEG_INF = -1e30

OFF_AQ = 0
OFF_AK = OFF_AQ + MOBA_WIDTH
OFF_AV = OFF_AK + MOBA_WIDTH
OFF_RQ = OFF_AV + MOBA_WIDTH
OFF_RK = OFF_RQ + RET_QK_WIDTH
OFF_RV = OFF_RK + RET_QK_WIDTH
OFF_RG = OFF_RV + RET_V_WIDTH
OFF_MQ = OFF_RG + RET_V_WIDTH
OFF_GA = OFF_MQ + MEM_WIDTH
OFF_GR = OFF_GA + D_MODEL
OFF_GM = OFF_GR + D_MODEL

LANES = 128
SUBLANES = 8
VMEM_LIMIT = 56 * 1024 * 1024

EXPERT_ROWS = 256
ROUTER_ROWS = 48
ROW_TILE = 256

_NT = (((1,), (1,)), ((), ()))


def _params(*sem):
    return pltpu.CompilerParams(dimension_semantics=sem, vmem_limit_bytes=VMEM_LIMIT)


def _mm_kernel(a_ref, w_ref, o_ref, wb_ref):
    @pl.when(pl.program_id(1) == 0)
    def _():
        wb_ref[...] = w_ref[...].astype(BF16)

    o_ref[...] = jnp.dot(a_ref[...], wb_ref[...], preferred_element_type=F32).astype(o_ref.dtype)


def _matmul(a, w, layer, out_dtype, tm, tn):
    m, k = a.shape
    n = w.shape[2]
    tm, tn = min(tm, m), min(tn, n)
    return pl.pallas_call(
        _mm_kernel,
        grid=(n // tn, m // tm),
        in_specs=[pl.BlockSpec((tm, k), lambda j, i: (i, 0)),
                  pl.BlockSpec((None, k, tn), lambda j, i: (layer, 0, j))],
        out_specs=pl.BlockSpec((tm, tn), lambda j, i: (i, j)),
        out_shape=jax.ShapeDtypeStruct((m, n), out_dtype),
        scratch_shapes=[pltpu.VMEM((k, tn), BF16)],
        compiler_params=_params("parallel", "arbitrary"),
        name="matmul",
    )(a, w)


def _moba_kernel(q_ref, k_ref, v_ref, o_ref, km_ref, vt_ref, sel_ref, *, n_blk, blk, top_n, scale):
    i = pl.program_id(2)
    seq = n_blk * blk
    dh = q_ref.shape[1]
    shift = blk.bit_length() - 1

    @pl.when(i == 0)
    def _():
        row = lax.broadcasted_iota(jnp.int32, (2 * SUBLANES, seq), 0)
        col_blk = lax.shift_right_logical(lax.broadcasted_iota(jnp.int32, (2 * SUBLANES, seq), 1), shift)
        ind = jnp.where(col_blk == (row & (SUBLANES - 1)), 1.0, 0.0).astype(BF16)
        kmean = jnp.dot(ind, k_ref[...], preferred_element_type=F32) * (1.0 / blk)
        hi = kmean.astype(BF16).astype(F32)
        r2 = lax.broadcasted_iota(jnp.int32, (2 * SUBLANES, dh), 0)
        km_ref[...] = jnp.where(r2 < SUBLANES, hi, kmean - hi).astype(BF16)
        for j in range(n_blk):
            vt_ref[j] = v_ref[j * blk:(j + 1) * blk, :].astype(F32).T.astype(BF16)

    q = q_ref[...]
    g16 = lax.dot_general(km_ref[...], q, _NT, preferred_element_type=F32)
    g = g16[0:SUBLANES] + g16[SUBLANES:2 * SUBLANES]
    jrow = lax.broadcasted_iota(jnp.int32, (SUBLANES, blk), 0)
    past = jrow < i
    g = jnp.where(past, g, NEG_INF)
    rank = jnp.zeros((SUBLANES, blk), F32)
    for jp in range(SUBLANES):
        gj = g[jp:jp + 1, :]
        beats = jnp.where(gj > g, 1.0, jnp.where((gj == g) & (jp < jrow), 1.0, 0.0))
        rank = rank + beats
    sel_ref[...] = jnp.where(past & (rank < top_n), 1.0, 0.0)

    k_own = k_ref[pl.ds(pl.multiple_of(i * blk, blk), blk), :]
    s = lax.dot_general(k_own, q, _NT, preferred_element_type=F32) * scale
    kpos = lax.broadcasted_iota(jnp.int32, (blk, blk), 0)
    qpos = lax.broadcasted_iota(jnp.int32, (blk, blk), 1)
    s = jnp.where(kpos <= qpos, s, NEG_INF)
    m0 = jnp.max(s, axis=0, keepdims=True)
    p = jnp.exp(s - m0)
    l0 = jnp.sum(p, axis=0, keepdims=True)
    acc0 = jnp.dot(vt_ref[i], p.astype(BF16), preferred_element_type=F32)

    def body(j, carry):
        m, l, acc = carry
        kj = k_ref[pl.ds(pl.multiple_of(j * blk, blk), blk), :]
        sj = lax.dot_general(kj, q, _NT, preferred_element_type=F32) * scale
        sj = jnp.where(sel_ref[pl.ds(j, 1), :] > 0.5, sj, NEG_INF)
        m_new = jnp.maximum(m, jnp.max(sj, axis=0, keepdims=True))
        alpha = jnp.exp(m - m_new)
        pj = jnp.exp(sj - m_new)
        l = alpha * l + jnp.sum(pj, axis=0, keepdims=True)
        acc = alpha * acc + jnp.dot(vt_ref[j], pj.astype(BF16), preferred_element_type=F32)
        return m_new, l, acc

    _, l, acc = lax.fori_loop(0, i, body, (m0, l0, acc0))
    o_ref[...] = (acc * (1.0 / l)).T.astype(o_ref.dtype)


def _moba(proj, bsz, seq):
    n_blk = seq // MOBA_BLOCK
    assert seq % MOBA_BLOCK == 0 and n_blk <= SUBLANES
    dh = MOBA_HEAD_DIM
    kern = functools.partial(_moba_kernel, n_blk=n_blk, blk=MOBA_BLOCK, top_n=min(MOBA_TOPK, n_blk),
                             scale=dh ** -0.5)
    return pl.pallas_call(
        kern,
        grid=(bsz, MOBA_HEADS, n_blk),
        in_specs=[pl.BlockSpec((MOBA_BLOCK, dh), lambda b, h, i: (b * n_blk + i, OFF_AQ // dh + h)),
                  pl.BlockSpec((seq, dh), lambda b, h, i: (b, OFF_AK // dh + h)),
                  pl.BlockSpec((seq, dh), lambda b, h, i: (b, OFF_AV // dh + h))],
        out_specs=pl.BlockSpec((MOBA_BLOCK, dh), lambda b, h, i: (b * n_blk + i, h)),
        out_shape=jax.ShapeDtypeStruct((bsz * seq, MOBA_WIDTH), BF16),
        scratch_shapes=[pltpu.VMEM((2 * SUBLANES, dh), BF16),
                        pltpu.VMEM((n_blk, dh, MOBA_BLOCK), BF16),
                        pltpu.VMEM((SUBLANES, MOBA_BLOCK), F32)],
        compiler_params=_params("parallel", "parallel", "arbitrary"),
        name="moba",
    )(proj, proj, proj)


def _ret_kernel(cd_ref, q_ref, k_ref, v_ref, g_ref, cos_ref, sin_ref, dec_ref, zeta_ref, xi_ref,
                o_ref, st_ref, *, dk):
    h = pl.program_id(1)
    n = pl.program_id(2)

    @pl.when(n == 0)
    def _():
        st_ref[...] = jnp.zeros_like(st_ref)

    half = dk // 2
    cos = cos_ref[...]
    sin = sin_ref[...]

    def rot(x):
        x1 = x[:, :half]
        x2 = x[:, half:]
        return jnp.concatenate([x1 * cos - x2 * sin, x1 * sin + x2 * cos], axis=1)

    def widen(t):
        return jnp.concatenate([t] * (dk // LANES), axis=1)

    q = rot(q_ref[...].astype(F32))
    k = rot(k_ref[...].astype(F32)) * (dk ** -0.5)
    v = v_ref[...]
    sc = lax.dot_general(q.astype(BF16), k.astype(BF16), _NT, preferred_element_type=F32) * dec_ref[0]
    inner = jnp.dot(sc.astype(BF16), v, preferred_element_type=F32)
    st = st_ref[...]
    cross = jnp.dot((q * widen(xi_ref[0])).astype(BF16), st.astype(BF16), preferred_element_type=F32)
    o = inner + cross
    kz = k * widen(zeta_ref[0])
    kv = jnp.dot(kz.T.astype(BF16), v, preferred_element_type=F32)
    st_ref[...] = cd_ref[h] * st + kv

    mu = jnp.mean(o, axis=1, keepdims=True)
    d = o - mu
    var = jnp.mean(d * d, axis=1, keepdims=True)
    gg = g_ref[...].astype(F32)
    o_ref[...] = (gg * jax.nn.sigmoid(gg) * (d * lax.rsqrt(var + GN_EPS))).astype(o_ref.dtype)


def _retention(proj, bsz, seq):
    c, dk, dv, nh = RET_CHUNK, RET_KEY_DIM, RET_VALUE_DIM, RET_HEADS
    n_c = seq // c
    half = dk // 2
    pos = jnp.arange(seq, dtype=F32)
    inv = ROPE_BASE ** (-jnp.linspace(0.0, 1.0, half, dtype=F32))
    ang = pos[:, None] * inv[None, :]
    cos, sin = jnp.cos(ang), jnp.sin(ang)
    log_g = jnp.log1p(-jnp.power(2.0, -5.0 - jnp.arange(nh, dtype=F32)))
    idx = jnp.arange(c, dtype=F32)
    diff = idx[:, None] - idx[None, :]
    decay = jnp.where(diff >= 0, jnp.exp(log_g[:, None, None] * jnp.maximum(diff, 0.0)), 0.0)
    zeta = jnp.exp(log_g[:, None] * (c - 1 - idx)[None, :])
    xi = jnp.exp(log_g[:, None] * (idx + 1.0)[None, :])
    zeta_b = jnp.broadcast_to(zeta[:, :, None], (nh, c, LANES))
    xi_b = jnp.broadcast_to(xi[:, :, None], (nh, c, LANES))
    chunk_decay = jnp.exp(log_g * c)

    return pl.pallas_call(
        functools.partial(_ret_kernel, dk=dk),
        grid=(bsz, nh, n_c),
        in_specs=[pl.BlockSpec(memory_space=pltpu.SMEM),
                  pl.BlockSpec((c, dk), lambda b, h, n: (b * n_c + n, OFF_RQ // dk + h)),
                  pl.BlockSpec((c, dk), lambda b, h, n: (b * n_c + n, OFF_RK // dk + h)),
                  pl.BlockSpec((c, dv), lambda b, h, n: (b * n_c + n, OFF_RV // dv + h)),
                  pl.BlockSpec((c, dv), lambda b, h, n: (b * n_c + n, OFF_RG // dv + h)),
                  pl.BlockSpec((c, half), lambda b, h, n: (n, 0)),
                  pl.BlockSpec((c, half), lambda b, h, n: (n, 0)),
                  pl.BlockSpec((1, c, c), lambda b, h, n: (h, 0, 0)),
                  pl.BlockSpec((1, c, LANES), lambda b, h, n: (h, 0, 0)),
                  pl.BlockSpec((1, c, LANES), lambda b, h, n: (h, 0, 0))],
        out_specs=pl.BlockSpec((c, dv), lambda b, h, n: (b * n_c + n, h)),
        out_shape=jax.ShapeDtypeStruct((bsz * seq, RET_V_WIDTH), BF16),
        scratch_shapes=[pltpu.VMEM((dk, dv), F32)],
        compiler_params=_params("parallel", "parallel", "arbitrary"),
        name="retention",
    )(chunk_decay, proj, proj, proj, proj, cos, sin, decay, zeta_b, xi_b)


def _mem_kernel(q_ref, k_ref, v_ref, o_ref, *, scale):
    s = lax.dot_general(q_ref[...], k_ref[...], _NT, preferred_element_type=F32) * scale
    m = jnp.max(s, axis=1, keepdims=True)
    p = jnp.exp(s - m)
    p = p * (1.0 / jnp.sum(p, axis=1, keepdims=True))
    o_ref[...] = jnp.dot(p.astype(BF16), v_ref[...], preferred_element_type=F32).astype(o_ref.dtype)


def _mem_attention(proj, kvm, bsz, seq, mem_len):
    dh, nh = MEM_HEAD_DIM, MEM_HEADS
    tq = min(512, seq)
    n_q = seq // tq
    return pl.pallas_call(
        functools.partial(_mem_kernel, scale=dh ** -0.5),
        grid=(bsz, nh, n_q),
        in_specs=[pl.BlockSpec((tq, dh), lambda b, h, i: (b * n_q + i, OFF_MQ // dh + h)),
                  pl.BlockSpec((mem_len, dh), lambda b, h, i: (b, h)),
                  pl.BlockSpec((mem_len, dh), lambda b, h, i: (b, nh + h))],
        out_specs=pl.BlockSpec((tq, dh), lambda b, h, i: (b * n_q + i, h)),
        out_shape=jax.ShapeDtypeStruct((bsz * seq, MEM_WIDTH), BF16),
        compiler_params=_params("parallel", "parallel", "parallel"),
        name="mem_attention",
    )(proj, kvm, kvm)


def _merge_kernel(ya_ref, yr_ref, ym_ref, pa_ref, pr_ref, pm_ref, ga_ref, gr_ref, gm_ref, o_ref,
                  pab_ref, prb_ref, pmb_ref):
    @pl.when(pl.program_id(1) == 0)
    def _():
        pab_ref[...] = pa_ref[...].astype(BF16)
        prb_ref[...] = pr_ref[...].astype(BF16)
        pmb_ref[...] = pm_ref[...].astype(BF16)

    def branch(y_ref, p_ref, gate_ref):
        t = jnp.dot(y_ref[...], p_ref[...], preferred_element_type=F32)
        return jax.nn.sigmoid(gate_ref[...].astype(F32)) * t

    o_ref[...] = (branch(ya_ref, pab_ref, ga_ref) + branch(yr_ref, prb_ref, gr_ref)
                  + branch(ym_ref, pmb_ref, gm_ref)).astype(o_ref.dtype)


def _merge(y_a, y_r, y_m, p_moba, p_ret, p_mem, layer, proj):
    m = y_a.shape[0]
    tm, tn = min(512, m), 512
    d = D_MODEL

    def rows(width):
        return pl.BlockSpec((tm, width), lambda j, i: (i, 0))

    def wcols(kdim):
        return pl.BlockSpec((None, kdim, tn), lambda j, i: (layer, 0, j))

    def gate(off):
        return pl.BlockSpec((tm, tn), lambda j, i: (i, off // tn + j))

    return pl.pallas_call(
        _merge_kernel,
        grid=(d // tn, m // tm),
        in_specs=[rows(MOBA_WIDTH), rows(RET_V_WIDTH), rows(MEM_WIDTH),
                  wcols(MOBA_WIDTH), wcols(RET_V_WIDTH), wcols(MEM_WIDTH),
                  gate(OFF_GA), gate(OFF_GR), gate(OFF_GM)],
        out_specs=pl.BlockSpec((tm, tn), lambda j, i: (i, j)),
        out_shape=jax.ShapeDtypeStruct((m, d), BF16),
        scratch_shapes=[pltpu.VMEM((MOBA_WIDTH, tn), BF16), pltpu.VMEM((RET_V_WIDTH, tn), BF16),
                        pltpu.VMEM((MEM_WIDTH, tn), BF16)],
        compiler_params=_params("parallel", "arbitrary"),
        name="merge",
    )(y_a, y_r, y_m, p_moba, p_ret, p_mem, proj, proj, proj)


def _layer_norm_rows(z, g, b):
    mu = jnp.mean(z, axis=1, keepdims=True)
    d = z - mu
    var = jnp.mean(d * d, axis=1, keepdims=True)
    return d * lax.rsqrt(var + LN_EPS) * g + b


def _wo_ln_router_kernel(mg_ref, wo_ref, x_ref, g_ref, b_ref, wr2_ref, wr1_ref, br_ref,
                         x1_ref, ids_ref, wts_ref, wob_ref):
    @pl.when(pl.program_id(0) == 0)
    def _():
        wob_ref[...] = wo_ref[...].astype(BF16)

    mix = jnp.dot(mg_ref[...], wob_ref[...], preferred_element_type=F32)
    x1 = _layer_norm_rows(DEEPNORM_ALPHA * x_ref[...] + mix, g_ref[...], b_ref[...])
    x1_ref[...] = x1

    r = ROUTER_ROWS
    x_hi = x1.astype(BF16)
    x_lo = (x1 - x_hi.astype(F32)).astype(BF16)
    l2 = lax.dot_general(wr2_ref[...], x_hi, _NT, preferred_element_type=F32)
    l1 = lax.dot_general(wr1_ref[...], x_lo, _NT, preferred_element_type=F32)
    logit = l2[0:r] + l2[r:2 * r] + l1 + br_ref[...]

    tok = logit.shape[1]
    row = lax.broadcasted_iota(jnp.int32, (SUBLANES, tok), 0).astype(F32)
    gl = jnp.where(row < N_GROUPS, logit[0:SUBLANES], NEG_INF)
    g_max = jnp.max(gl, axis=0, keepdims=True)
    g_sel = jnp.min(jnp.where(gl == g_max, row, float(SUBLANES)), axis=0, keepdims=True)
    p_g = 1.0 / jnp.sum(jnp.exp(gl - g_max), axis=0, keepdims=True)
    e_in = jnp.zeros((SUBLANES, tok), F32)
    for grp in range(N_GROUPS):
        lo = SUBLANES * (1 + grp)
        e_in = jnp.where(g_sel == float(grp), logit[lo:lo + EXPERTS_PER_GROUP], e_in)
    v1 = jnp.max(e_in, axis=0, keepdims=True)
    i1 = jnp.min(jnp.where(e_in == v1, row, float(SUBLANES)), axis=0, keepdims=True)
    e_rest = jnp.where(row == i1, -jnp.inf, e_in)
    v2 = jnp.max(e_rest, axis=0, keepdims=True)
    i2 = jnp.min(jnp.where(e_rest == v2, row, float(SUBLANES)), axis=0, keepdims=True)
    t = jnp.exp(v2 - v1)
    w1 = p_g / (1.0 + t)
    w2 = w1 * t
    e1 = g_sel * float(EXPERTS_PER_GROUP) + i1
    e2 = g_sel * float(EXPERTS_PER_GROUP) + i2
    ids_ref[...] = jnp.where(row == 0.0, e1, jnp.where(row == 1.0, e2, 0.0)).astype(jnp.int32)
    wts_ref[...] = jnp.where(row == 0.0, w1, jnp.where(row == 1.0, w2, 0.0))


def _wo_ln_router(merged, w_o, layer, x, ln_g, ln_b, w_group, b_group, w_expert, b_expert):
    m, d = x.shape
    tm = min(ROW_TILE, m)
    r = ROUTER_ROWS
    wr = jnp.zeros((r, d), F32).at[0:N_GROUPS].set(w_group.T).at[SUBLANES:SUBLANES + N_EXPERTS].set(w_expert.T)
    br = jnp.zeros((r, 1), F32).at[0:N_GROUPS, 0].set(b_group).at[SUBLANES:SUBLANES + N_EXPERTS, 0].set(b_expert)
    wr_hi = wr.astype(BF16)
    wr_lo = (wr - wr_hi.astype(F32)).astype(BF16)
    wr2 = jnp.concatenate([wr_hi, wr_lo], axis=0)

    def whole(shape):
        return pl.BlockSpec(shape, lambda i: (0,) * len(shape))

    return pl.pallas_call(
        _wo_ln_router_kernel,
        grid=(m // tm,),
        in_specs=[pl.BlockSpec((tm, d), lambda i: (i, 0)),
                  pl.BlockSpec((None, d, d), lambda i: (layer, 0, 0), pipeline_mode=pl.Buffered(1)),
                  pl.BlockSpec((tm, d), lambda i: (i, 0)),
                  whole((1, d)), whole((1, d)),
                  whole((2 * r, d)), whole((r, d)), whole((r, 1))],
        out_specs=[pl.BlockSpec((tm, d), lambda i: (i, 0)),
                   pl.BlockSpec((SUBLANES, tm), lambda i: (0, i)),
                   pl.BlockSpec((SUBLANES, tm), lambda i: (0, i))],
        out_shape=[jax.ShapeDtypeStruct((m, d), F32),
                   jax.ShapeDtypeStruct((SUBLANES, m), jnp.int32),
                   jax.ShapeDtypeStruct((SUBLANES, m), F32)],
        scratch_shapes=[pltpu.VMEM((d, d), BF16)],
        compiler_params=_params("arbitrary"),
        name="wo_ln_router",
    )(merged, w_o, x, ln_g.reshape(1, d), ln_b.reshape(1, d), wr2, wr_hi, br)


def _row_copy(src, src_row, dst, dst_row, sem):
    return pltpu.make_async_copy(src.at[pl.ds(src_row, 1)], dst.at[pl.ds(dst_row, 1)], sem)


def _rank_kernel(e_ref, pos_ref, cnt_ref, tri_ref, carry_ref):
    n = e_ref.shape[1]

    @pl.when(pl.program_id(0) == 0)
    def _():
        r = lax.broadcasted_iota(jnp.int32, (n, n), 0)
        c = lax.broadcasted_iota(jnp.int32, (n, n), 1)
        tri_ref[...] = jnp.where(r <= c, 1.0, 0.0).astype(BF16)
        carry_ref[...] = jnp.zeros_like(carry_ref)

    row = lax.broadcasted_iota(jnp.int32, (N_EXPERTS, n), 0)
    onehot = jnp.where(row == e_ref[...], 1.0, 0.0)
    prefix = jnp.dot(onehot.astype(BF16), tri_ref[...], preferred_element_type=F32)
    carry = carry_ref[...]
    pos = jnp.sum(onehot * (prefix + carry[:, 0:1]), axis=0, keepdims=True) - 1.0
    pos_ref[...] = pos.astype(jnp.int32)
    carry = carry + jnp.sum(onehot, axis=1, keepdims=True)
    carry_ref[...] = carry
    cnt_ref[...] = carry


def _rank(flat_e):
    n_asg = flat_e.shape[1]
    tile = min(512, n_asg)
    return pl.pallas_call(
        _rank_kernel,
        grid=(n_asg // tile,),
        in_specs=[pl.BlockSpec((1, tile), lambda i: (0, i))],
        out_specs=[pl.BlockSpec((1, tile), lambda i: (0, i)),
                   pl.BlockSpec((N_EXPERTS, LANES), lambda i: (0, 0))],
        out_shape=[jax.ShapeDtypeStruct((1, n_asg), jnp.int32),
                   jax.ShapeDtypeStruct((N_EXPERTS, LANES), F32)],
        scratch_shapes=[pltpu.VMEM((tile, tile), BF16), pltpu.VMEM((N_EXPERTS, LANES), F32)],
        compiler_params=_params("arbitrary"),
        name="rank",
    )(flat_e)


def _invert_kernel(d0_ref, d1_ref, o_ref):
    def clear(r, c):
        o_ref[r] = 0
        return c

    lax.fori_loop(0, o_ref.shape[0], clear, 0, unroll=8)

    def put(t, c):
        o_ref[d0_ref[t]] = t
        o_ref[d1_ref[t]] = t
        return c

    lax.fori_loop(0, d0_ref.shape[0], put, 0, unroll=4)


def _invert(dest0, dest1, n_pad):
    smem = pl.BlockSpec(memory_space=pltpu.SMEM)
    return pl.pallas_call(
        _invert_kernel,
        in_specs=[smem, smem],
        out_specs=smem,
        out_shape=jax.ShapeDtypeStruct((n_pad,), jnp.int32),
        name="invert",
    )(dest0, dest1)


def _expert_kernel(be_ref, nu_ref, rt_ref, x_hbm, wgu_ref, wd_ref, o_ref, xbuf, wgu_b, wd_b, sem,
                   *, rows, n_blocks):
    i = pl.program_id(0)
    nu = nu_ref[0]
    ff = wd_ref.shape[0]

    def issue(blk, slot):
        base = blk * rows
        for r in range(rows):
            _row_copy(x_hbm, rt_ref[base + r], xbuf.at[slot], r, sem.at[slot]).start()

    def drain(slot):
        for _ in range(rows):
            _row_copy(x_hbm, 0, xbuf.at[slot], 0, sem.at[slot]).wait()

    @pl.when(i == 0)
    def _():
        issue(0, 0)

    @pl.when((i < nu) & ((i == 0) | (be_ref[i] != be_ref[jnp.maximum(i - 1, 0)])))
    def _():
        wgu_b[...] = wgu_ref[...].astype(BF16)
        wd_b[...] = wd_ref[...].astype(BF16)

    def compute(slot):
        drain(slot)
        xb = xbuf[slot].astype(BF16)
        issue(jnp.minimum(i + 1, n_blocks - 1), 1 - slot)
        gu = jnp.dot(xb, wgu_b[...], preferred_element_type=F32)
        gate = gu[:, :ff]
        hmid = gate * jax.nn.sigmoid(gate) * gu[:, ff:]
        o_ref[...] = jnp.dot(hmid.astype(BF16), wd_b[...], preferred_element_type=F32)

    for slot in range(2):
        @pl.when((i < nu) & (i % 2 == slot))
        def _(slot=slot):
            compute(slot)

    for slot in range(2):
        @pl.when(((i == nu) & (i % 2 == slot)) | ((i == n_blocks - 1) & (i < nu) & (i % 2 != slot)))
        def _(slot=slot):
            drain(slot)

    @pl.when(i >= nu_ref[0])
    def _():
        o_ref[...] = jnp.zeros_like(o_ref)


def _experts(x1, row_tok, block_e, n_used, w_gate_up, w_down, layer):
    d = x1.shape[1]
    n_pad = row_tok.shape[0]
    ff = w_down.shape[2]
    rows = EXPERT_ROWS
    n_blocks = n_pad // rows
    return pl.pallas_call(
        functools.partial(_expert_kernel, rows=rows, n_blocks=n_blocks),
        grid_spec=pltpu.PrefetchScalarGridSpec(
            num_scalar_prefetch=3,
            grid=(n_blocks,),
            in_specs=[pl.BlockSpec(memory_space=pl.ANY),
                      pl.BlockSpec((None, None, d, 2 * ff), lambda i, be, nu, rt: (layer, be[i], 0, 0)),
                      pl.BlockSpec((None, None, ff, d), lambda i, be, nu, rt: (layer, be[i], 0, 0))],
            out_specs=pl.BlockSpec((rows, d), lambda i, be, nu, rt: (i, 0)),
            scratch_shapes=[pltpu.VMEM((2, rows, d), F32),
                            pltpu.VMEM((d, 2 * ff), BF16),
                            pltpu.VMEM((ff, d), BF16),
                            pltpu.SemaphoreType.DMA((2,))]),
        out_shape=jax.ShapeDtypeStruct((n_pad, d), F32),
        compiler_params=_params("arbitrary"),
        name="experts",
    )(block_e, n_used, row_tok, x1, w_gate_up, w_down)


def _combine_ln_kernel(d0_ref, d1_ref, y_hbm, w_ref, x_ref, g_ref, b_ref, o_ref, ob_ref, buf, sem,
                       *, tb, n_steps):
    i = pl.program_id(0)

    def issue(step, slot):
        base = step * tb
        for t in range(tb):
            _row_copy(y_hbm, d0_ref[base + t], buf.at[slot, 0], t, sem.at[slot]).start()
            _row_copy(y_hbm, d1_ref[base + t], buf.at[slot, 1], t, sem.at[slot]).start()

    def drain(slot):
        for _ in range(EXPERT_TOPK * tb):
            _row_copy(y_hbm, 0, buf.at[slot, 0], 0, sem.at[slot]).wait()

    @pl.when(i == 0)
    def _():
        issue(0, 0)

    def compute(slot):
        drain(slot)
        w = w_ref[...]
        ffn = w[:, 0:1] * buf[slot, 0] + w[:, 1:2] * buf[slot, 1]
        out = _layer_norm_rows(DEEPNORM_ALPHA * x_ref[...] + ffn, g_ref[...], b_ref[...])
        o_ref[...] = out
        ob_ref[...] = out.astype(BF16)

    for slot in range(2):
        @pl.when((i % 2 == slot) & (i + 1 < n_steps))
        def _(slot=slot):
            issue(i + 1, 1 - slot)

        @pl.when(i % 2 == slot)
        def _(slot=slot):
            compute(slot)


def _combine_ln(y_rows, dest0, dest1, w_tok, x1, ln_g, ln_b):
    m, d = x1.shape
    tb = min(128, m)
    n_steps = m // tb
    return pl.pallas_call(
        functools.partial(_combine_ln_kernel, tb=tb, n_steps=n_steps),
        grid_spec=pltpu.PrefetchScalarGridSpec(
            num_scalar_prefetch=2,
            grid=(n_steps,),
            in_specs=[pl.BlockSpec(memory_space=pl.ANY),
                      pl.BlockSpec((tb, EXPERT_TOPK), lambda i, a, b: (i, 0)),
                      pl.BlockSpec((tb, d), lambda i, a, b: (i, 0)),
                      pl.BlockSpec((1, d), lambda i, a, b: (0, 0)),
                      pl.BlockSpec((1, d), lambda i, a, b: (0, 0))],
            out_specs=[pl.BlockSpec((tb, d), lambda i, a, b: (i, 0)),
                       pl.BlockSpec((tb, d), lambda i, a, b: (i, 0))],
            scratch_shapes=[pltpu.VMEM((2, EXPERT_TOPK, tb, d), F32), pltpu.SemaphoreType.DMA((2,))]),
        out_shape=[jax.ShapeDtypeStruct((m, d), F32), jax.ShapeDtypeStruct((m, d), BF16)],
        compiler_params=_params("arbitrary"),
        name="combine_ln",
    )(dest0, dest1, y_rows, w_tok, x1, ln_g.reshape(1, d), ln_b.reshape(1, d))


def _slot_plan(ids, n_tok):
    rows = EXPERT_ROWS
    n_asg = n_tok * EXPERT_TOPK
    n_pad = n_asg + N_EXPERTS * rows
    flat_e = ids[:EXPERT_TOPK].reshape(1, n_asg)
    pos, cnt = _rank(flat_e)
    counts = cnt[:, 0].astype(jnp.int32)
    padded = ((counts + rows - 1) // rows) * rows
    pad_ends = jnp.cumsum(padded)
    pad_starts = pad_ends - padded
    experts = jnp.arange(N_EXPERTS, dtype=jnp.int32)
    start_of = jnp.sum(jnp.where(flat_e[0][:, None] == experts[None, :], pad_starts[None, :], 0), axis=1)
    dest = (start_of + pos[0]).astype(jnp.int32)
    n_blocks = n_pad // rows
    block_start = jnp.arange(n_blocks, dtype=jnp.int32) * rows
    block_e = jnp.minimum(jnp.sum((pad_ends[None, :] <= block_start[:, None]).astype(jnp.int32), axis=1),
                          N_EXPERTS - 1)
    n_used = (pad_ends[-1:] // rows).astype(jnp.int32)
    dest0, dest1 = dest[:n_tok], dest[n_tok:]
    return dest0, dest1, _invert(dest0, dest1, n_pad), block_e, n_used


def kernel(x, mem, w_in, p_moba, p_ret, p_mem, w_mem_kv, w_o, ln1_g, ln1_b, w_group, b_group,
           w_expert, b_expert, w_gate_up, w_down, ln2_g, ln2_b):
    bsz, seq, d = x.shape
    mem_len = mem.shape[1]
    n_tok = bsz * seq
    xf = x.reshape(n_tok, d)
    xb = xf.astype(BF16)
    mem_b = mem.reshape(bsz * mem_len, d).astype(BF16)
    for l in range(w_in.shape[0]):
        proj = _matmul(xb, w_in, l, BF16, 512, 1024)
        kvm = _matmul(mem_b, w_mem_kv, l, BF16, 512, 1024)
        y_a = _moba(proj, bsz, seq)
        y_r = _retention(proj, bsz, seq)
        y_m = _mem_attention(proj, kvm, bsz, seq, mem_len)
        merged = _merge(y_a, y_r, y_m, p_moba, p_ret, p_mem, l, proj)
        x1, ids, wts = _wo_ln_router(merged, w_o, l, xf, ln1_g[l], ln1_b[l],
                                     w_group[l], b_group[l], w_expert[l], b_expert[l])
        dest0, dest1, row_tok, block_e, n_used = _slot_plan(ids, n_tok)
        y_rows = _experts(x1, row_tok, block_e, n_used, w_gate_up, w_down, l)
        xf, xb = _combine_ln(y_rows, dest0, dest1, wts[:EXPERT_TOPK].T, x1, ln2_g[l], ln2_b[l])
    return xf.reshape(bsz, seq, d)
```

```python
import functools

import jax
import jax.numpy as jnp
from jax import lax
from jax.experimental import pallas as pl
from jax.experimental.pallas import tpu as pltpu

F32 = jnp.float32
BF16 = jnp.bfloat16

D_MODEL = 2048
DEPTH = 2
MOBA_HEADS = 8
MOBA_HEAD_DIM = 128
MOBA_BLOCK = 256
MOBA_TOPK = 3
RET_HEADS = 4
RET_KEY_DIM = 256
RET_VALUE_DIM = 512
RET_CHUNK = 128
ROPE_BASE = 10000.0
MEM_HEADS = 4
MEM_HEAD_DIM = 256
N_BRANCHES = 3
MOBA_WIDTH = MOBA_HEADS * MOBA_HEAD_DIM
RET_QK_WIDTH = RET_HEADS * RET_KEY_DIM
RET_V_WIDTH = RET_HEADS * RET_VALUE_DIM
MEM_WIDTH = MEM_HEADS * MEM_HEAD_DIM
N_GROUPS = 4
EXPERTS_PER_GROUP = 8
N_EXPERTS = N_GROUPS * EXPERTS_PER_GROUP
EXPERT_TOPK = 2
EXPERT_FF = 512
DEEPNORM_ALPHA = (2 * DEPTH) ** 0.25
LN_EPS = 1e-5
GN_EPS = 1e-6
NEG_INF = -1e30

OFF_AQ = 0
OFF_AK = OFF_AQ + MOBA_WIDTH
OFF_AV = OFF_AK + MOBA_WIDTH
OFF_RQ = OFF_AV + MOBA_WIDTH
OFF_RK = OFF_RQ + RET_QK_WIDTH
OFF_RV = OFF_RK + RET_QK_WIDTH
OFF_RG = OFF_RV + RET_V_WIDTH
OFF_MQ = OFF_RG + RET_V_WIDTH
OFF_GA = OFF_MQ + MEM_WIDTH
OFF_GR = OFF_GA + D_MODEL
OFF_GM = OFF_GR + D_MODEL

LANES = 128
SUBLANES = 8
VMEM_LIMIT = 56 * 1024 * 1024

EXPERT_ROWS = 256
ROUTER_ROWS = 48
ROW_TILE = 256

_NT = (((1,), (1,)), ((), ()))


def _params(*sem):
    return pltpu.CompilerParams(dimension_semantics=sem, vmem_limit_bytes=VMEM_LIMIT)


def _mm_kernel(a_ref, w_ref, o_ref, wb_ref):
    @pl.when(pl.program_id(1) == 0)
    def _():
        wb_ref[...] = w_ref[...].astype(BF16)

    o_ref[...] = jnp.dot(a_ref[...], wb_ref[...], preferred_element_type=F32).astype(o_ref.dtype)


def _matmul(a, w, layer, out_dtype, tm, tn):
    m, k = a.shape
    n = w.shape[2]
    tm, tn = min(tm, m), min(tn, n)
    return pl.pallas_call(
        _mm_kernel,
        grid=(n // tn, m // tm),
        in_specs=[pl.BlockSpec((tm, k), lambda j, i: (i, 0)),
                  pl.BlockSpec((None, k, tn), lambda j, i: (layer, 0, j))],
        out_specs=pl.BlockSpec((tm, tn), lambda j, i: (i, j)),
        out_shape=jax.ShapeDtypeStruct((m, n), out_dtype),
        scratch_shapes=[pltpu.VMEM((k, tn), BF16)],
        compiler_params=_params("parallel", "arbitrary"),
        name="matmul",
    )(a, w)


def _moba_kernel(q_ref, k_ref, v_ref, o_ref, vt_ref, *, n_blk, blk, top_n, scale):
    seq = n_blk * blk
    dh = q_ref.shape[1]
    shift = blk.bit_length() - 1

    row = lax.broadcasted_iota(jnp.int32, (2 * SUBLANES, seq), 0)
    col_blk = lax.shift_right_logical(lax.broadcasted_iota(jnp.int32, (2 * SUBLANES, seq), 1), shift)
    ind = jnp.where(col_blk == (row & (SUBLANES - 1)), 1.0, 0.0).astype(BF16)
    kmean = jnp.dot(ind, k_ref[...], preferred_element_type=F32) * (1.0 / blk)
    hi = kmean.astype(BF16).astype(F32)
    r2 = lax.broadcasted_iota(jnp.int32, (2 * SUBLANES, dh), 0)
    km = jnp.where(r2 < SUBLANES, hi, kmean - hi).astype(BF16)
    for j in range(n_blk):
        vt_ref[:, j * blk:(j + 1) * blk] = v_ref[j * blk:(j + 1) * blk, :].astype(F32).T.astype(BF16)

    g16 = lax.dot_general(km, q_ref[...], _NT, preferred_element_type=F32)
    g = g16[0:SUBLANES] + g16[SUBLANES:2 * SUBLANES]
    jrow = lax.broadcasted_iota(jnp.int32, (SUBLANES, seq), 0)
    q_blk = lax.shift_right_logical(lax.broadcasted_iota(jnp.int32, (SUBLANES, seq), 1), shift)
    past = jrow < q_blk
    g = jnp.where(past, g, NEG_INF)
    rank = jnp.zeros((SUBLANES, seq), F32)
    for jp in range(SUBLANES):
        gj = g[jp:jp + 1, :]
        rank = rank + jnp.where(gj > g, 1.0, jnp.where((gj == g) & (jp < jrow), 1.0, 0.0))
    sel = jnp.where(past & (rank < top_n), 1.0, 0.0)

    causal = (lax.broadcasted_iota(jnp.int32, (blk, blk), 0)
              <= lax.broadcasted_iota(jnp.int32, (blk, blk), 1))
    for i in range(n_blk):
        lo, hi_q = i * blk, (i + 1) * blk
        s = lax.dot_general(k_ref[0:hi_q, :], q_ref[lo:hi_q, :], _NT,
                            preferred_element_type=F32) * scale
        parts = [jnp.where(sel[j:j + 1, lo:hi_q] > 0.5, s[j * blk:(j + 1) * blk], NEG_INF) for j in range(i)]
        parts.append(jnp.where(causal, s[lo:hi_q], NEG_INF))
        m = functools.reduce(jnp.maximum, [jnp.max(p, axis=0, keepdims=True) for p in parts])
        probs = [jnp.exp(p - m) for p in parts]
        denom = functools.reduce(jnp.add, [jnp.sum(p, axis=0, keepdims=True) for p in probs])
        p_all = jnp.concatenate([p.astype(BF16) for p in probs], axis=0)
        acc = jnp.dot(vt_ref[:, 0:hi_q], p_all, preferred_element_type=F32)
        o_ref[lo:hi_q, :] = (acc * (1.0 / denom)).T.astype(o_ref.dtype)


def _moba(proj, bsz, seq):
    n_blk = seq // MOBA_BLOCK
    assert seq % MOBA_BLOCK == 0 and n_blk <= SUBLANES
    dh = MOBA_HEAD_DIM
    kern = functools.partial(_moba_kernel, n_blk=n_blk, blk=MOBA_BLOCK, top_n=min(MOBA_TOPK, n_blk),
                             scale=dh ** -0.5)
    return pl.pallas_call(
        kern,
        grid=(bsz, MOBA_HEADS),
        in_specs=[pl.BlockSpec((seq, dh), lambda b, h: (b, OFF_AQ // dh + h)),
                  pl.BlockSpec((seq, dh), lambda b, h: (b, OFF_AK // dh + h)),
                  pl.BlockSpec((seq, dh), lambda b, h: (b, OFF_AV // dh + h))],
        out_specs=pl.BlockSpec((seq, dh), lambda b, h: (b, h)),
        out_shape=jax.ShapeDtypeStruct((bsz * seq, MOBA_WIDTH), BF16),
        scratch_shapes=[pltpu.VMEM((dh, seq), BF16)],
        compiler_params=_params("parallel", "parallel"),
        name="moba",
    )(proj, proj, proj)


def _ret_kernel(cd_ref, q_ref, k_ref, v_ref, g_ref, cos_ref, sin_ref, dec_ref, zeta_ref, xi_ref,
                o_ref, st_ref, *, dk, c):
    cd = cd_ref[pl.program_id(1)]
    n_c = q_ref.shape[0] // c
    half = dk // 2

    def widen(t):
        return jnp.concatenate([t] * (dk // LANES), axis=1)

    dec = dec_ref[...]
    xi = widen(xi_ref[...])
    zeta = widen(zeta_ref[...])

    for n in range(n_c):
        r = slice(n * c, (n + 1) * c)
        cos = cos_ref[r, :]
        sin = sin_ref[r, :]

        def rot(x):
            x1 = x[:, :half]
            x2 = x[:, half:]
            return jnp.concatenate([x1 * cos - x2 * sin, x1 * sin + x2 * cos], axis=1)

        q = rot(q_ref[r, :].astype(F32))
        k = rot(k_ref[r, :].astype(F32)) * (dk ** -0.5)
        v = v_ref[r, :]
        sc = lax.dot_general(q.astype(BF16), k.astype(BF16), _NT, preferred_element_type=F32) * dec
        o = jnp.dot(sc.astype(BF16), v, preferred_element_type=F32)
        if n > 0:
            o = o + jnp.dot((q * xi).astype(BF16), st_ref[...].astype(BF16), preferred_element_type=F32)
        if n < n_c - 1:
            kv = jnp.dot((k * zeta).T.astype(BF16), v, preferred_element_type=F32)
            st_ref[...] = kv if n == 0 else cd * st_ref[...] + kv

        mu = jnp.mean(o, axis=1, keepdims=True)
        d = o - mu
        var = jnp.mean(d * d, axis=1, keepdims=True)
        gg = g_ref[r, :].astype(F32)
        o_ref[r, :] = (gg * jax.nn.sigmoid(gg) * (d * lax.rsqrt(var + GN_EPS))).astype(o_ref.dtype)


def _retention(proj, bsz, seq):
    c, dk, dv, nh = RET_CHUNK, RET_KEY_DIM, RET_VALUE_DIM, RET_HEADS
    n_c = seq // c
    half = dk // 2
    pos = jnp.arange(seq, dtype=F32)
    inv = ROPE_BASE ** (-jnp.linspace(0.0, 1.0, half, dtype=F32))
    ang = pos[:, None] * inv[None, :]
    cos, sin = jnp.cos(ang), jnp.sin(ang)
    log_g = jnp.log1p(-jnp.power(2.0, -5.0 - jnp.arange(nh, dtype=F32)))
    idx = jnp.arange(c, dtype=F32)
    diff = idx[:, None] - idx[None, :]
    decay = jnp.where(diff >= 0, jnp.exp(log_g[:, None, None] * jnp.maximum(diff, 0.0)), 0.0)
    zeta = jnp.exp(log_g[:, None] * (c - 1 - idx)[None, :])
    xi = jnp.exp(log_g[:, None] * (idx + 1.0)[None, :])
    zeta_b = jnp.broadcast_to(zeta[:, :, None], (nh, c, LANES))
    xi_b = jnp.broadcast_to(xi[:, :, None], (nh, c, LANES))
    chunk_decay = jnp.exp(log_g * c)

    return pl.pallas_call(
        functools.partial(_ret_kernel, dk=dk, c=c),
        grid=(bsz, nh),
        in_specs=[pl.BlockSpec(memory_space=pltpu.SMEM),
                  pl.BlockSpec((seq, dk), lambda b, h: (b, OFF_RQ // dk + h)),
                  pl.BlockSpec((seq, dk), lambda b, h: (b, OFF_RK // dk + h)),
                  pl.BlockSpec((seq, dv), lambda b, h: (b, OFF_RV // dv + h)),
                  pl.BlockSpec((seq, dv), lambda b, h: (b, OFF_RG // dv + h)),
                  pl.BlockSpec((seq, half), lambda b, h: (0, 0)),
                  pl.BlockSpec((seq, half), lambda b, h: (0, 0)),
                  pl.BlockSpec((None, c, c), lambda b, h: (h, 0, 0)),
                  pl.BlockSpec((None, c, LANES), lambda b, h: (h, 0, 0)),
                  pl.BlockSpec((None, c, LANES), lambda b, h: (h, 0, 0))],
        out_specs=pl.BlockSpec((seq, dv), lambda b, h: (b, h)),
        out_shape=jax.ShapeDtypeStruct((bsz * seq, RET_V_WIDTH), BF16),
        scratch_shapes=[pltpu.VMEM((dk, dv), F32)],
        compiler_params=_params("parallel", "parallel"),
        name="retention",
    )(chunk_decay, proj, proj, proj, proj, cos, sin, decay, zeta_b, xi_b)


def _mem_kernel(q_ref, k_ref, v_ref, o_ref, *, scale):
    s = lax.dot_general(q_ref[...], k_ref[...], _NT, preferred_element_type=F32) * scale
    m = jnp.max(s, axis=1, keepdims=True)
    p = jnp.exp(s - m)
    p = p * (1.0 / jnp.sum(p, axis=1, keepdims=True))
    o_ref[...] = jnp.dot(p.astype(BF16), v_ref[...], preferred_element_type=F32).astype(o_ref.dtype)


def _mem_attention(proj, kvm, bsz, seq, mem_len):
    dh, nh = MEM_HEAD_DIM, MEM_HEADS
    tq = min(512, seq)
    n_q = seq // tq
    return pl.pallas_call(
        functools.partial(_mem_kernel, scale=dh ** -0.5),
        grid=(bsz, nh, n_q),
        in_specs=[pl.BlockSpec((tq, dh), lambda b, h, i: (b * n_q + i, OFF_MQ // dh + h)),
                  pl.BlockSpec((mem_len, dh), lambda b, h, i: (b, h)),
                  pl.BlockSpec((mem_len, dh), lambda b, h, i: (b, nh + h))],
        out_specs=pl.BlockSpec((tq, dh), lambda b, h, i: (b * n_q + i, h)),
        out_shape=jax.ShapeDtypeStruct((bsz * seq, MEM_WIDTH), BF16),
        compiler_params=_params("parallel", "parallel", "parallel"),
        name="mem_attention",
    )(proj, kvm, kvm)


def _merge_kernel(ya_ref, yr_ref, ym_ref, pa_ref, pr_ref, pm_ref, ga_ref, gr_ref, gm_ref, o_ref,
                  pab_ref, prb_ref, pmb_ref):
    @pl.when(pl.program_id(1) == 0)
    def _():
        pab_ref[...] = pa_ref[...].astype(BF16)
        prb_ref[...] = pr_ref[...].astype(BF16)
        pmb_ref[...] = pm_ref[...].astype(BF16)

    def branch(y_ref, p_ref, gate_ref):
        t = jnp.dot(y_ref[...], p_ref[...], preferred_element_type=F32)
        return jax.nn.sigmoid(gate_ref[...].astype(F32)) * t

    o_ref[...] = (branch(ya_ref, pab_ref, ga_ref) + branch(yr_ref, prb_ref, gr_ref)
                  + branch(ym_ref, pmb_ref, gm_ref)).astype(o_ref.dtype)


def _merge(y_a, y_r, y_m, p_moba, p_ret, p_mem, layer, proj):
    m = y_a.shape[0]
    tm, tn = min(512, m), 512
    d = D_MODEL

    def rows(width):
        return pl.BlockSpec((tm, width), lambda j, i: (i, 0))

    def wcols(kdim):
        return pl.BlockSpec((None, kdim, tn), lambda j, i: (layer, 0, j))

    def gate(off):
        return pl.BlockSpec((tm, tn), lambda j, i: (i, off // tn + j))

    return pl.pallas_call(
        _merge_kernel,
        grid=(d // tn, m // tm),
        in_specs=[rows(MOBA_WIDTH), rows(RET_V_WIDTH), rows(MEM_WIDTH),
                  wcols(MOBA_WIDTH), wcols(RET_V_WIDTH), wcols(MEM_WIDTH),
                  gate(OFF_GA), gate(OFF_GR), gate(OFF_GM)],
        out_specs=pl.BlockSpec((tm, tn), lambda j, i: (i, j)),
        out_shape=jax.ShapeDtypeStruct((m, d), BF16),
        scratch_shapes=[pltpu.VMEM((MOBA_WIDTH, tn), BF16), pltpu.VMEM((RET_V_WIDTH, tn), BF16),
                        pltpu.VMEM((MEM_WIDTH, tn), BF16)],
        compiler_params=_params("parallel", "arbitrary"),
        name="merge",
    )(y_a, y_r, y_m, p_moba, p_ret, p_mem, proj, proj, proj)


def _layer_norm_rows(z, g, b):
    mu = jnp.mean(z, axis=1, keepdims=True)
    d = z - mu
    var = jnp.mean(d * d, axis=1, keepdims=True)
    return d * lax.rsqrt(var + LN_EPS) * g + b


def _wo_ln_router_kernel(mg_ref, wo_ref, x_ref, g_ref, b_ref, wr2_ref, wr1_ref, br_ref,
                         x1_ref, ids_ref, wts_ref, wob_ref):
    @pl.when(pl.program_id(0) == 0)
    def _():
        wob_ref[...] = wo_ref[...].astype(BF16)

    mix = jnp.dot(mg_ref[...], wob_ref[...], preferred_element_type=F32)
    x1 = _layer_norm_rows(DEEPNORM_ALPHA * x_ref[...] + mix, g_ref[...], b_ref[...])
    x1_ref[...] = x1

    r = ROUTER_ROWS
    x_hi = x1.astype(BF16)
    x_lo = (x1 - x_hi.astype(F32)).astype(BF16)
    l2 = lax.dot_general(wr2_ref[...], x_hi, _NT, preferred_element_type=F32)
    l1 = lax.dot_general(wr1_ref[...], x_lo, _NT, preferred_element_type=F32)
    logit = l2[0:r] + l2[r:2 * r] + l1 + br_ref[...]

    tok = logit.shape[1]
    row = lax.broadcasted_iota(jnp.int32, (SUBLANES, tok), 0).astype(F32)
    gl = jnp.where(row < N_GROUPS, logit[0:SUBLANES], NEG_INF)
    g_max = jnp.max(gl, axis=0, keepdims=True)
    g_sel = jnp.min(jnp.where(gl == g_max, row, float(SUBLANES)), axis=0, keepdims=True)
    p_g = 1.0 / jnp.sum(jnp.exp(gl - g_max), axis=0, keepdims=True)
    e_in = jnp.zeros((SUBLANES, tok), F32)
    for grp in range(N_GROUPS):
        lo = SUBLANES * (1 + grp)
        e_in = jnp.where(g_sel == float(grp), logit[lo:lo + EXPERTS_PER_GROUP], e_in)
    v1 = jnp.max(e_in, axis=0, keepdims=True)
    i1 = jnp.min(jnp.where(e_in == v1, row, float(SUBLANES)), axis=0, keepdims=True)
    e_rest = jnp.where(row == i1, -jnp.inf, e_in)
    v2 = jnp.max(e_rest, axis=0, keepdims=True)
    i2 = jnp.min(jnp.where(e_rest == v2, row, float(SUBLANES)), axis=0, keepdims=True)
    t = jnp.exp(v2 - v1)
    w1 = p_g / (1.0 + t)
    w2 = w1 * t
    e1 = g_sel * float(EXPERTS_PER_GROUP) + i1
    e2 = g_sel * float(EXPERTS_PER_GROUP) + i2
    ids_ref[...] = jnp.where(row == 0.0, e1, jnp.where(row == 1.0, e2, 0.0)).astype(jnp.int32)
    wts_ref[...] = jnp.where(row == 0.0, w1, jnp.where(row == 1.0, w2, 0.0))


def _wo_ln_router(merged, w_o, layer, x, ln_g, ln_b, w_group, b_group, w_expert, b_expert):
    m, d = x.shape
    tm = min(ROW_TILE, m)
    r = ROUTER_ROWS
    wr = jnp.zeros((r, d), F32).at[0:N_GROUPS].set(w_group.T).at[SUBLANES:SUBLANES + N_EXPERTS].set(w_expert.T)
    br = jnp.zeros((r, 1), F32).at[0:N_GROUPS, 0].set(b_group).at[SUBLANES:SUBLANES + N_EXPERTS, 0].set(b_expert)
    wr_hi = wr.astype(BF16)
    wr_lo = (wr - wr_hi.astype(F32)).astype(BF16)
    wr2 = jnp.concatenate([wr_hi, wr_lo], axis=0)

    def whole(shape):
        return pl.BlockSpec(shape, lambda i: (0,) * len(shape))

    return pl.pallas_call(
        _wo_ln_router_kernel,
        grid=(m // tm,),
        in_specs=[pl.BlockSpec((tm, d), lambda i: (i, 0)),
                  pl.BlockSpec((None, d, d), lambda i: (layer, 0, 0), pipeline_mode=pl.Buffered(1)),
                  pl.BlockSpec((tm, d), lambda i: (i, 0)),
                  whole((1, d)), whole((1, d)),
                  whole((2 * r, d)), whole((r, d)), whole((r, 1))],
        out_specs=[pl.BlockSpec((tm, d), lambda i: (i, 0)),
                   pl.BlockSpec((SUBLANES, tm), lambda i: (0, i)),
                   pl.BlockSpec((SUBLANES, tm), lambda i: (0, i))],
        out_shape=[jax.ShapeDtypeStruct((m, d), F32),
                   jax.ShapeDtypeStruct((SUBLANES, m), jnp.int32),
                   jax.ShapeDtypeStruct((SUBLANES, m), F32)],
        scratch_shapes=[pltpu.VMEM((d, d), BF16)],
        compiler_params=_params("arbitrary"),
        name="wo_ln_router",
    )(merged, w_o, x, ln_g.reshape(1, d), ln_b.reshape(1, d), wr2, wr_hi, br)


def _row_copy(src, src_row, dst, dst_row, sem):
    return pltpu.make_async_copy(src.at[pl.ds(src_row, 1)], dst.at[pl.ds(dst_row, 1)], sem)


def _rank_kernel(e_ref, pos_ref, cnt_ref, tri_ref, carry_ref):
    n = e_ref.shape[1]

    @pl.when(pl.program_id(0) == 0)
    def _():
        r = lax.broadcasted_iota(jnp.int32, (n, n), 0)
        c = lax.broadcasted_iota(jnp.int32, (n, n), 1)
        tri_ref[...] = jnp.where(r <= c, 1.0, 0.0).astype(BF16)
        carry_ref[...] = jnp.zeros_like(carry_ref)

    row = lax.broadcasted_iota(jnp.int32, (N_EXPERTS, n), 0)
    onehot = jnp.where(row == e_ref[...], 1.0, 0.0)
    prefix = jnp.dot(onehot.astype(BF16), tri_ref[...], preferred_element_type=F32)
    carry = carry_ref[...]
    pos = jnp.sum(onehot * (prefix + carry[:, 0:1]), axis=0, keepdims=True) - 1.0
    pos_ref[...] = pos.astype(jnp.int32)
    carry = carry + jnp.sum(onehot, axis=1, keepdims=True)
    carry_ref[...] = carry
    cnt_ref[...] = carry


def _rank(flat_e):
    n_asg = flat_e.shape[1]
    tile = min(512, n_asg)
    return pl.pallas_call(
        _rank_kernel,
        grid=(n_asg // tile,),
        in_specs=[pl.BlockSpec((1, tile), lambda i: (0, i))],
        out_specs=[pl.BlockSpec((1, tile), lambda i: (0, i)),
                   pl.BlockSpec((N_EXPERTS, LANES), lambda i: (0, 0))],
        out_shape=[jax.ShapeDtypeStruct((1, n_asg), jnp.int32),
                   jax.ShapeDtypeStruct((N_EXPERTS, LANES), F32)],
        scratch_shapes=[pltpu.VMEM((tile, tile), BF16), pltpu.VMEM((N_EXPERTS, LANES), F32)],
        compiler_params=_params("arbitrary"),
        name="rank",
    )(flat_e)


def _invert_kernel(d0_ref, d1_ref, o_ref):
    def clear(r, c):
        o_ref[r] = 0
        return c

    lax.fori_loop(0, o_ref.shape[0], clear, 0, unroll=8)

    def put(t, c):
        o_ref[d0_ref[t]] = t
        o_ref[d1_ref[t]] = t
        return c

    lax.fori_loop(0, d0_ref.shape[0], put, 0, unroll=4)


def _invert(dest0, dest1, n_pad):
    smem = pl.BlockSpec(memory_space=pltpu.SMEM)
    return pl.pallas_call(
        _invert_kernel,
        in_specs=[smem, smem],
        out_specs=smem,
        out_shape=jax.ShapeDtypeStruct((n_pad,), jnp.int32),
        name="invert",
    )(dest0, dest1)


GATHER_SLOTS = 3


def _expert_kernel(be_ref, nx_ref, nu_ref, rt_ref, x_hbm, wgu_hbm, wd_hbm, o_ref,
                   xbuf, wgu_s, wd_s, wgu_b, wd_b, gsem, wsem, *, rows, n_blocks, layer):
    i = pl.program_id(0)
    nu = nu_ref[0]
    ff = wd_b.shape[0]
    ns = GATHER_SLOTS

    def issue(blk, slot):
        base = blk * rows
        for r in range(rows):
            _row_copy(x_hbm, rt_ref[base + r], xbuf.at[slot], r, gsem.at[slot]).start()

    def drain(slot):
        for _ in range(rows):
            _row_copy(x_hbm, 0, xbuf.at[slot], 0, gsem.at[slot]).wait()

    def weight_copies(e):
        return (pltpu.make_async_copy(wgu_hbm.at[layer, e], wgu_s, wsem.at[0]),
                pltpu.make_async_copy(wd_hbm.at[layer, e], wd_s, wsem.at[1]))

    @pl.when(i == 0)
    def _():
        for cp in weight_copies(be_ref[0]):
            cp.start()
        issue(0, 0)
        issue(min(1, n_blocks - 1), 1)

    @pl.when((i < nu) & ((i == 0) | (be_ref[i] != be_ref[jnp.maximum(i - 1, 0)])))
    def _():
        for cp in weight_copies(be_ref[i]):
            cp.wait()
        wgu_b[...] = wgu_s[...].astype(BF16)
        wd_b[...] = wd_s[...].astype(BF16)

        @pl.when(nx_ref[i] >= 0)
        def _():
            for cp in weight_copies(nx_ref[i]):
                cp.start()

    def compute(slot):
        drain(slot)
        xb = xbuf[slot].astype(BF16)
        issue(jnp.minimum(i + 2, n_blocks - 1), (slot + 2) % ns)
        gu = jnp.dot(xb, wgu_b[...], preferred_element_type=F32)
        gate = gu[:, :ff]
        hmid = gate * jax.nn.sigmoid(gate) * gu[:, ff:]
        o_ref[...] = jnp.dot(hmid.astype(BF16), wd_b[...], preferred_element_type=F32)

    phase = lax.rem(i, ns)
    for slot in range(ns):
        @pl.when((i < nu) & (phase == slot))
        def _(slot=slot):
            compute(slot)

    for slot in range(ns):
        @pl.when((i == nu) & (phase == slot))
        def _(slot=slot):
            drain(slot)
            drain((slot + 1) % ns)

    @pl.when(i >= nu)
    def _():
        o_ref[...] = jnp.zeros_like(o_ref)


def _experts(x1, row_tok, block_e, next_e, n_used, w_gate_up, w_down, layer):
    d = x1.shape[1]
    n_pad = row_tok.shape[0]
    ff = w_down.shape[2]
    rows = EXPERT_ROWS
    n_blocks = n_pad // rows
    assert n_blocks >= 2
    return pl.pallas_call(
        functools.partial(_expert_kernel, rows=rows, n_blocks=n_blocks, layer=layer),
        grid_spec=pltpu.PrefetchScalarGridSpec(
            num_scalar_prefetch=4,
            grid=(n_blocks,),
            in_specs=[pl.BlockSpec(memory_space=pl.ANY),
                      pl.BlockSpec(memory_space=pl.ANY),
                      pl.BlockSpec(memory_space=pl.ANY)],
            out_specs=pl.BlockSpec((rows, d), lambda i, be, nx, nu, rt: (i, 0)),
            scratch_shapes=[pltpu.VMEM((GATHER_SLOTS, rows, d), F32),
                            pltpu.VMEM((d, 2 * ff), F32),
                            pltpu.VMEM((ff, d), F32),
                            pltpu.VMEM((d, 2 * ff), BF16),
                            pltpu.VMEM((ff, d), BF16),
                            pltpu.SemaphoreType.DMA((GATHER_SLOTS,)),
                            pltpu.SemaphoreType.DMA((2,))]),
        out_shape=jax.ShapeDtypeStruct((n_pad, d), F32),
        compiler_params=_params("arbitrary"),
        name="experts",
    )(block_e, next_e, n_used, row_tok, x1, w_gate_up, w_down)


def _combine_ln_kernel(d0_ref, d1_ref, y_hbm, w_ref, x_ref, g_ref, b_ref, o_ref, ob_ref, buf, sem,
                       *, tb, n_steps):
    i = pl.program_id(0)

    def issue(step, slot):
        base = step * tb
        for t in range(tb):
            _row_copy(y_hbm, d0_ref[base + t], buf.at[slot, 0], t, sem.at[slot]).start()
            _row_copy(y_hbm, d1_ref[base + t], buf.at[slot, 1], t, sem.at[slot]).start()

    def drain(slot):
        for _ in range(EXPERT_TOPK * tb):
            _row_copy(y_hbm, 0, buf.at[slot, 0], 0, sem.at[slot]).wait()

    @pl.when(i == 0)
    def _():
        issue(0, 0)

    def compute(slot):
        drain(slot)
        w = w_ref[...]
        ffn = w[:, 0:1] * buf[slot, 0] + w[:, 1:2] * buf[slot, 1]
        out = _layer_norm_rows(DEEPNORM_ALPHA * x_ref[...] + ffn, g_ref[...], b_ref[...])
        o_ref[...] = out
        ob_ref[...] = out.astype(BF16)

    for slot in range(2):
        @pl.when((i % 2 == slot) & (i + 1 < n_steps))
        def _(slot=slot):
            issue(i + 1, 1 - slot)

        @pl.when(i % 2 == slot)
        def _(slot=slot):
            compute(slot)


def _combine_ln(y_rows, dest0, dest1, w_tok, x1, ln_g, ln_b):
    m, d = x1.shape
    tb = min(128, m)
    n_steps = m // tb
    return pl.pallas_call(
        functools.partial(_combine_ln_kernel, tb=tb, n_steps=n_steps),
        grid_spec=pltpu.PrefetchScalarGridSpec(
            num_scalar_prefetch=2,
            grid=(n_steps,),
            in_specs=[pl.BlockSpec(memory_space=pl.ANY),
                      pl.BlockSpec((tb, EXPERT_TOPK), lambda i, a, b: (i, 0)),
                      pl.BlockSpec((tb, d), lambda i, a, b: (i, 0)),
                      pl.BlockSpec((1, d), lambda i, a, b: (0, 0)),
                      pl.BlockSpec((1, d), lambda i, a, b: (0, 0))],
            out_specs=[pl.BlockSpec((tb, d), lambda i, a, b: (i, 0)),
                       pl.BlockSpec((tb, d), lambda i, a, b: (i, 0))],
            scratch_shapes=[pltpu.VMEM((2, EXPERT_TOPK, tb, d), F32), pltpu.SemaphoreType.DMA((2,))]),
        out_shape=[jax.ShapeDtypeStruct((m, d), F32), jax.ShapeDtypeStruct((m, d), BF16)],
        compiler_params=_params("arbitrary"),
        name="combine_ln",
    )(dest0, dest1, y_rows, w_tok, x1, ln_g.reshape(1, d), ln_b.reshape(1, d))


def _slot_plan(ids, n_tok):
    rows = EXPERT_ROWS
    n_asg = n_tok * EXPERT_TOPK
    n_pad = n_asg + N_EXPERTS * rows
    flat_e = ids[:EXPERT_TOPK].reshape(1, n_asg)
    pos, cnt = _rank(flat_e)
    counts = cnt[:, 0].astype(jnp.int32)
    padded = ((counts + rows - 1) // rows) * rows
    pad_ends = jnp.cumsum(padded)
    pad_starts = pad_ends - padded
    experts = jnp.arange(N_EXPERTS, dtype=jnp.int32)
    start_of = jnp.sum(jnp.where(flat_e[0][:, None] == experts[None, :], pad_starts[None, :], 0), axis=1)
    dest = (start_of + pos[0]).astype(jnp.int32)
    n_blocks = n_pad // rows
    block_start = jnp.arange(n_blocks, dtype=jnp.int32) * rows
    block_e = jnp.minimum(jnp.sum((pad_ends[None, :] <= block_start[:, None]).astype(jnp.int32), axis=1),
                          N_EXPERTS - 1)
    later = (experts[None, :] > experts[:, None]) & (counts[None, :] > 0)
    next_of = jnp.min(jnp.where(later, experts[None, :], N_EXPERTS), axis=1)
    next_of = jnp.where(next_of == N_EXPERTS, -1, next_of)
    next_e = jnp.sum(jnp.where(block_e[:, None] == experts[None, :], next_of[None, :], 0), axis=1).astype(jnp.int32)
    n_used = (pad_ends[-1:] // rows).astype(jnp.int32)
    dest0, dest1 = dest[:n_tok], dest[n_tok:]
    return dest0, dest1, _invert(dest0, dest1, n_pad), block_e, next_e, n_used


def kernel(x, mem, w_in, p_moba, p_ret, p_mem, w_mem_kv, w_o, ln1_g, ln1_b, w_group, b_group,
           w_expert, b_expert, w_gate_up, w_down, ln2_g, ln2_b):
    bsz, seq, d = x.shape
    mem_len = mem.shape[1]
    n_tok = bsz * seq
    xf = x.reshape(n_tok, d)
    xb = xf.astype(BF16)
    mem_b = mem.reshape(bsz * mem_len, d).astype(BF16)
    for l in range(w_in.shape[0]):
        proj = _matmul(xb, w_in, l, BF16, 512, 1024)
        kvm = _matmul(mem_b, w_mem_kv, l, BF16, 512, 1024)
        y_a = _moba(proj, bsz, seq)
        y_r = _retention(proj, bsz, seq)
        y_m = _mem_attention(proj, kvm, bsz, seq, mem_len)
        merged = _merge(y_a, y_r, y_m, p_moba, p_ret, p_mem, l, proj)
        x1, ids, wts = _wo_ln_router(merged, w_o, l, xf, ln1_g[l], ln1_b[l],
                                     w_group[l], b_group[l], w_expert[l], b_expert[l])
        dest0, dest1, row_tok, block_e, next_e, n_used = _slot_plan(ids, n_tok)
        y_rows = _experts(x1, row_tok, block_e, next_e, n_used, w_gate_up, w_down, l)
        xf, xb = _combine_ln(y_rows, dest0, dest1, wts[:EXPERT_TOPK].T, x1, ln2_g[l], ln2_b[l])
    return xf.reshape(bsz, seq, d)
```

```python
import functools

import jax
import jax.numpy as jnp
from jax import lax
from jax.experimental import pallas as pl
from jax.experimental.pallas import tpu as pltpu

F32 = jnp.float32
BF16 = jnp.bfloat16

D_MODEL = 2048
DEPTH = 2
MOBA_HEADS = 8
MOBA_HEAD_DIM = 128
MOBA_BLOCK = 256
MOBA_TOPK = 3
RET_HEADS = 4
RET_KEY_DIM = 256
RET_VALUE_DIM = 512
RET_CHUNK = 128
ROPE_BASE = 10000.0
MEM_HEADS = 4
MEM_HEAD_DIM = 256
N_BRANCHES = 3
MOBA_WIDTH = MOBA_HEADS * MOBA_HEAD_DIM
RET_QK_WIDTH = RET_HEADS * RET_KEY_DIM
RET_V_WIDTH = RET_HEADS * RET_VALUE_DIM
MEM_WIDTH = MEM_HEADS * MEM_HEAD_DIM
N_GROUPS = 4
EXPERTS_PER_GROUP = 8
N_EXPERTS = N_GROUPS * EXPERTS_PER_GROUP
EXPERT_TOPK = 2
EXPERT_FF = 512
DEEPNORM_ALPHA = (2 * DEPTH) ** 0.25
LN_EPS = 1e-5
GN_EPS = 1e-6
NEG_INF = -1e30
LOG2_E = 1.4426950408889634

OFF_AQ = 0
OFF_AK = OFF_AQ + MOBA_WIDTH
OFF_AV = OFF_AK + MOBA_WIDTH
OFF_RQ = OFF_AV + MOBA_WIDTH
OFF_RK = OFF_RQ + RET_QK_WIDTH
OFF_RV = OFF_RK + RET_QK_WIDTH
OFF_RG = OFF_RV + RET_V_WIDTH
OFF_MQ = OFF_RG + RET_V_WIDTH
OFF_GA = OFF_MQ + MEM_WIDTH
OFF_GR = OFF_GA + D_MODEL
OFF_GM = OFF_GR + D_MODEL

LANES = 128
SUBLANES = 8
VMEM_LIMIT = 56 * 1024 * 1024

EXPERT_ROWS = 256
ROUTER_ROWS = 48
ROW_TILE = 256

_NT = (((1,), (1,)), ((), ()))


def _params(*sem):
    return pltpu.CompilerParams(dimension_semantics=sem, vmem_limit_bytes=VMEM_LIMIT)


def _mm_kernel(a_ref, w_ref, o_ref, wb_ref):
    @pl.when(pl.program_id(1) == 0)
    def _():
        wb_ref[...] = w_ref[...].astype(BF16)

    o_ref[...] = jnp.dot(a_ref[...], wb_ref[...], preferred_element_type=F32).astype(o_ref.dtype)


def _matmul(a, w, layer, out_dtype, tm, tn):
    m, k = a.shape
    n = w.shape[2]
    tm, tn = min(tm, m), min(tn, n)
    return pl.pallas_call(
        _mm_kernel,
        grid=(n // tn, m // tm),
        in_specs=[pl.BlockSpec((tm, k), lambda j, i: (i, 0)),
                  pl.BlockSpec((None, k, tn), lambda j, i: (layer, 0, j))],
        out_specs=pl.BlockSpec((tm, tn), lambda j, i: (i, j)),
        out_shape=jax.ShapeDtypeStruct((m, n), out_dtype),
        scratch_shapes=[pltpu.VMEM((k, tn), BF16)],
        compiler_params=_params("parallel", "arbitrary"),
        name="matmul",
    )(a, w)


def _moba_kernel(q_ref, k_ref, v_ref, o_ref, vt_ref, *, n_blk, blk, top_n, scale):
    seq = n_blk * blk
    dh = q_ref.shape[1]
    shift = blk.bit_length() - 1

    row = lax.broadcasted_iota(jnp.int32, (2 * SUBLANES, seq), 0)
    col_blk = lax.shift_right_logical(lax.broadcasted_iota(jnp.int32, (2 * SUBLANES, seq), 1), shift)
    ind = jnp.where(col_blk == (row & (SUBLANES - 1)), 1.0, 0.0).astype(BF16)
    kmean = jnp.dot(ind, k_ref[...], preferred_element_type=F32) * (1.0 / blk)
    hi = kmean.astype(BF16).astype(F32)
    r2 = lax.broadcasted_iota(jnp.int32, (2 * SUBLANES, dh), 0)
    km = jnp.where(r2 < SUBLANES, hi, kmean - hi).astype(BF16)
    for j in range(n_blk):
        vt_ref[:, j * blk:(j + 1) * blk] = v_ref[j * blk:(j + 1) * blk, :].astype(F32).T.astype(BF16)

    g16 = lax.dot_general(km, q_ref[...], _NT, preferred_element_type=F32)
    g = g16[0:SUBLANES] + g16[SUBLANES:2 * SUBLANES]
    jrow = lax.broadcasted_iota(jnp.int32, (SUBLANES, seq), 0)
    q_blk = lax.shift_right_logical(lax.broadcasted_iota(jnp.int32, (SUBLANES, seq), 1), shift)
    past = jrow < q_blk
    g = jnp.where(past, g, NEG_INF)
    rank = jnp.zeros((SUBLANES, seq), F32)
    for jp in range(SUBLANES):
        gj = g[jp:jp + 1, :]
        rank = rank + jnp.where(gj > g, 1.0, jnp.where((gj == g) & (jp < jrow), 1.0, 0.0))
    sel = jnp.where(past & (rank < top_n), 1.0, 0.0)

    causal = (lax.broadcasted_iota(jnp.int32, (blk, blk), 0)
              <= lax.broadcasted_iota(jnp.int32, (blk, blk), 1))
    for i in range(n_blk):
        lo, hi_q = i * blk, (i + 1) * blk
        s = lax.dot_general(k_ref[0:hi_q, :], q_ref[lo:hi_q, :], _NT,
                            preferred_element_type=F32)
        parts = [jnp.where(sel[j:j + 1, lo:hi_q] > 0.5, s[j * blk:(j + 1) * blk], NEG_INF) for j in range(i)]
        parts.append(jnp.where(causal, s[lo:hi_q], NEG_INF))
        m = functools.reduce(jnp.maximum, [jnp.max(p, axis=0, keepdims=True) for p in parts])
        probs = [jnp.exp2((p - m) * (scale * LOG2_E)) for p in parts]
        denom = functools.reduce(jnp.add, [jnp.sum(p, axis=0, keepdims=True) for p in probs])
        p_all = jnp.concatenate([p.astype(BF16) for p in probs], axis=0)
        acc = jnp.dot(vt_ref[:, 0:hi_q], p_all, preferred_element_type=F32)
        o_ref[lo:hi_q, :] = (acc * (1.0 / denom)).T.astype(o_ref.dtype)


def _moba(proj, bsz, seq):
    n_blk = seq // MOBA_BLOCK
    assert seq % MOBA_BLOCK == 0 and n_blk <= SUBLANES
    dh = MOBA_HEAD_DIM
    kern = functools.partial(_moba_kernel, n_blk=n_blk, blk=MOBA_BLOCK, top_n=min(MOBA_TOPK, n_blk),
                             scale=dh ** -0.5)
    return pl.pallas_call(
        kern,
        grid=(bsz, MOBA_HEADS),
        in_specs=[pl.BlockSpec((seq, dh), lambda b, h: (b, OFF_AQ // dh + h)),
                  pl.BlockSpec((seq, dh), lambda b, h: (b, OFF_AK // dh + h)),
                  pl.BlockSpec((seq, dh), lambda b, h: (b, OFF_AV // dh + h))],
        out_specs=pl.BlockSpec((seq, dh), lambda b, h: (b, h)),
        out_shape=jax.ShapeDtypeStruct((bsz * seq, MOBA_WIDTH), BF16),
        scratch_shapes=[pltpu.VMEM((dh, seq), BF16)],
        compiler_params=_params("parallel", "parallel"),
        name="moba",
    )(proj, proj, proj)


def _ret_kernel(cd_ref, q_ref, k_ref, v_ref, g_ref, cos_ref, sin_ref, dec_ref, zeta_ref, xi_ref,
                o_ref, st_ref, *, dk, c):
    cd = cd_ref[pl.program_id(1)]
    n_c = q_ref.shape[0] // c
    half = dk // 2

    def widen(t):
        return jnp.concatenate([t] * (dk // LANES), axis=1)

    dec = dec_ref[...]
    xi = widen(xi_ref[...])
    zeta = widen(zeta_ref[...])

    for n in range(n_c):
        r = slice(n * c, (n + 1) * c)
        cos = cos_ref[r, :]
        sin = sin_ref[r, :]

        def rot(x):
            x1 = x[:, :half]
            x2 = x[:, half:]
            return jnp.concatenate([x1 * cos - x2 * sin, x1 * sin + x2 * cos], axis=1)

        q = rot(q_ref[r, :].astype(F32))
        k = rot(k_ref[r, :].astype(F32)) * (dk ** -0.5)
        v = v_ref[r, :]
        sc = lax.dot_general(q.astype(BF16), k.astype(BF16), _NT, preferred_element_type=F32) * dec
        o = jnp.dot(sc.astype(BF16), v, preferred_element_type=F32)
        if n > 0:
            o = o + jnp.dot((q * xi).astype(BF16), st_ref[...].astype(BF16), preferred_element_type=F32)
        if n < n_c - 1:
            kv = jnp.dot((k * zeta).T.astype(BF16), v, preferred_element_type=F32)
            st_ref[...] = kv if n == 0 else cd * st_ref[...] + kv

        mu = jnp.mean(o, axis=1, keepdims=True)
        d = o - mu
        var = jnp.mean(d * d, axis=1, keepdims=True)
        gg = g_ref[r, :].astype(F32)
        o_ref[r, :] = (gg * jax.nn.sigmoid(gg) * (d * lax.rsqrt(var + GN_EPS))).astype(o_ref.dtype)


def _retention(proj, bsz, seq):
    c, dk, dv, nh = RET_CHUNK, RET_KEY_DIM, RET_VALUE_DIM, RET_HEADS
    n_c = seq // c
    half = dk // 2
    pos = jnp.arange(seq, dtype=F32)
    inv = ROPE_BASE ** (-jnp.linspace(0.0, 1.0, half, dtype=F32))
    ang = pos[:, None] * inv[None, :]
    cos, sin = jnp.cos(ang), jnp.sin(ang)
    log_g = jnp.log1p(-jnp.power(2.0, -5.0 - jnp.arange(nh, dtype=F32)))
    idx = jnp.arange(c, dtype=F32)
    diff = idx[:, None] - idx[None, :]
    decay = jnp.where(diff >= 0, jnp.exp(log_g[:, None, None] * jnp.maximum(diff, 0.0)), 0.0)
    zeta = jnp.exp(log_g[:, None] * (c - 1 - idx)[None, :])
    xi = jnp.exp(log_g[:, None] * (idx + 1.0)[None, :])
    zeta_b = jnp.broadcast_to(zeta[:, :, None], (nh, c, LANES))
    xi_b = jnp.broadcast_to(xi[:, :, None], (nh, c, LANES))
    chunk_decay = jnp.exp(log_g * c)

    return pl.pallas_call(
        functools.partial(_ret_kernel, dk=dk, c=c),
        grid=(bsz, nh),
        in_specs=[pl.BlockSpec(memory_space=pltpu.SMEM),
                  pl.BlockSpec((seq, dk), lambda b, h: (b, OFF_RQ // dk + h)),
                  pl.BlockSpec((seq, dk), lambda b, h: (b, OFF_RK // dk + h)),
                  pl.BlockSpec((seq, dv), lambda b, h: (b, OFF_RV // dv + h)),
                  pl.BlockSpec((seq, dv), lambda b, h: (b, OFF_RG // dv + h)),
                  pl.BlockSpec((seq, half), lambda b, h: (0, 0)),
                  pl.BlockSpec((seq, half), lambda b, h: (0, 0)),
                  pl.BlockSpec((None, c, c), lambda b, h: (h, 0, 0)),
                  pl.BlockSpec((None, c, LANES), lambda b, h: (h, 0, 0)),
                  pl.BlockSpec((None, c, LANES), lambda b, h: (h, 0, 0))],
        out_specs=pl.BlockSpec((seq, dv), lambda b, h: (b, h)),
        out_shape=jax.ShapeDtypeStruct((bsz * seq, RET_V_WIDTH), BF16),
        scratch_shapes=[pltpu.VMEM((dk, dv), F32)],
        compiler_params=_params("parallel", "parallel"),
        name="retention",
    )(chunk_decay, proj, proj, proj, proj, cos, sin, decay, zeta_b, xi_b)


def _mem_kernel(q_ref, k_ref, v_ref, o_ref, *, scale):
    s = lax.dot_general(q_ref[...], k_ref[...], _NT, preferred_element_type=F32) * scale
    m = jnp.max(s, axis=1, keepdims=True)
    p = jnp.exp(s - m)
    p = p * (1.0 / jnp.sum(p, axis=1, keepdims=True))
    o_ref[...] = jnp.dot(p.astype(BF16), v_ref[...], preferred_element_type=F32).astype(o_ref.dtype)


def _mem_attention(proj, kvm, bsz, seq, mem_len):
    dh, nh = MEM_HEAD_DIM, MEM_HEADS
    tq = min(2048, seq)
    n_q = seq // tq
    return pl.pallas_call(
        functools.partial(_mem_kernel, scale=dh ** -0.5),
        grid=(bsz, nh, n_q),
        in_specs=[pl.BlockSpec((tq, dh), lambda b, h, i: (b * n_q + i, OFF_MQ // dh + h)),
                  pl.BlockSpec((mem_len, dh), lambda b, h, i: (b, h)),
                  pl.BlockSpec((mem_len, dh), lambda b, h, i: (b, nh + h))],
        out_specs=pl.BlockSpec((tq, dh), lambda b, h, i: (b * n_q + i, h)),
        out_shape=jax.ShapeDtypeStruct((bsz * seq, MEM_WIDTH), BF16),
        compiler_params=_params("parallel", "parallel", "parallel"),
        name="mem_attention",
    )(proj, kvm, kvm)


def _merge_kernel(ya_ref, yr_ref, ym_ref, pa_ref, pr_ref, pm_ref, ga_ref, gr_ref, gm_ref, o_ref,
                  pab_ref, prb_ref, pmb_ref):
    @pl.when(pl.program_id(1) == 0)
    def _():
        pab_ref[...] = pa_ref[...].astype(BF16)
        prb_ref[...] = pr_ref[...].astype(BF16)
        pmb_ref[...] = pm_ref[...].astype(BF16)

    def branch(y_ref, p_ref, gate_ref):
        t = jnp.dot(y_ref[...], p_ref[...], preferred_element_type=F32)
        return jax.nn.sigmoid(gate_ref[...].astype(F32)) * t

    o_ref[...] = (branch(ya_ref, pab_ref, ga_ref) + branch(yr_ref, prb_ref, gr_ref)
                  + branch(ym_ref, pmb_ref, gm_ref)).astype(o_ref.dtype)


def _merge(y_a, y_r, y_m, p_moba, p_ret, p_mem, layer, proj):
    m = y_a.shape[0]
    tm, tn = min(512, m), 1024
    d = D_MODEL

    def rows(width):
        return pl.BlockSpec((tm, width), lambda j, i: (i, 0))

    def wcols(kdim):
        return pl.BlockSpec((None, kdim, tn), lambda j, i: (layer, 0, j), pipeline_mode=pl.Buffered(1))

    def gate(off):
        return pl.BlockSpec((tm, tn), lambda j, i: (i, off // tn + j))

    return pl.pallas_call(
        _merge_kernel,
        grid=(d // tn, m // tm),
        in_specs=[rows(MOBA_WIDTH), rows(RET_V_WIDTH), rows(MEM_WIDTH),
                  wcols(MOBA_WIDTH), wcols(RET_V_WIDTH), wcols(MEM_WIDTH),
                  gate(OFF_GA), gate(OFF_GR), gate(OFF_GM)],
        out_specs=pl.BlockSpec((tm, tn), lambda j, i: (i, j)),
        out_shape=jax.ShapeDtypeStruct((m, d), BF16),
        scratch_shapes=[pltpu.VMEM((MOBA_WIDTH, tn), BF16), pltpu.VMEM((RET_V_WIDTH, tn), BF16),
                        pltpu.VMEM((MEM_WIDTH, tn), BF16)],
        compiler_params=_params("parallel", "arbitrary"),
        name="merge",
    )(y_a, y_r, y_m, p_moba, p_ret, p_mem, proj, proj, proj)


def _layer_norm_rows(z, g, b):
    mu = jnp.mean(z, axis=1, keepdims=True)
    d = z - mu
    var = jnp.mean(d * d, axis=1, keepdims=True)
    return d * lax.rsqrt(var + LN_EPS) * g + b


def _wo_ln_router_kernel(mg_ref, wo_ref, x_ref, g_ref, b_ref, wr2_ref, wr1_ref, br_ref,
                         x1_ref, ids_ref, wts_ref, wob_ref):
    @pl.when(pl.program_id(0) == 0)
    def _():
        wob_ref[...] = wo_ref[...].astype(BF16)

    mix = jnp.dot(mg_ref[...], wob_ref[...], preferred_element_type=F32)
    x1 = _layer_norm_rows(DEEPNORM_ALPHA * x_ref[...] + mix, g_ref[...], b_ref[...])
    x1_ref[...] = x1

    r = ROUTER_ROWS
    x_hi = x1.astype(BF16)
    x_lo = (x1 - x_hi.astype(F32)).astype(BF16)
    l2 = lax.dot_general(wr2_ref[...], x_hi, _NT, preferred_element_type=F32)
    l1 = lax.dot_general(wr1_ref[...], x_lo, _NT, preferred_element_type=F32)
    logit = l2[0:r] + l2[r:2 * r] + l1 + br_ref[...]

    tok = logit.shape[1]
    row = lax.broadcasted_iota(jnp.int32, (SUBLANES, tok), 0).astype(F32)
    gl = jnp.where(row < N_GROUPS, logit[0:SUBLANES], NEG_INF)
    g_max = jnp.max(gl, axis=0, keepdims=True)
    g_sel = jnp.min(jnp.where(gl == g_max, row, float(SUBLANES)), axis=0, keepdims=True)
    p_g = 1.0 / jnp.sum(jnp.exp(gl - g_max), axis=0, keepdims=True)
    e_in = jnp.zeros((SUBLANES, tok), F32)
    for grp in range(N_GROUPS):
        lo = SUBLANES * (1 + grp)
        e_in = jnp.where(g_sel == float(grp), logit[lo:lo + EXPERTS_PER_GROUP], e_in)
    v1 = jnp.max(e_in, axis=0, keepdims=True)
    i1 = jnp.min(jnp.where(e_in == v1, row, float(SUBLANES)), axis=0, keepdims=True)
    e_rest = jnp.where(row == i1, -jnp.inf, e_in)
    v2 = jnp.max(e_rest, axis=0, keepdims=True)
    i2 = jnp.min(jnp.where(e_rest == v2, row, float(SUBLANES)), axis=0, keepdims=True)
    t = jnp.exp(v2 - v1)
    w1 = p_g / (1.0 + t)
    w2 = w1 * t
    e1 = g_sel * float(EXPERTS_PER_GROUP) + i1
    e2 = g_sel * float(EXPERTS_PER_GROUP) + i2
    ids_ref[...] = jnp.where(row == 0.0, e1, jnp.where(row == 1.0, e2, 0.0)).astype(jnp.int32)
    wts_ref[...] = jnp.where(row == 0.0, w1, jnp.where(row == 1.0, w2, 0.0))


def _wo_ln_router(merged, w_o, layer, x, ln_g, ln_b, w_group, b_group, w_expert, b_expert):
    m, d = x.shape
    tm = min(ROW_TILE, m)
    r = ROUTER_ROWS
    wr = jnp.zeros((r, d), F32).at[0:N_GROUPS].set(w_group.T).at[SUBLANES:SUBLANES + N_EXPERTS].set(w_expert.T)
    br = jnp.zeros((r, 1), F32).at[0:N_GROUPS, 0].set(b_group).at[SUBLANES:SUBLANES + N_EXPERTS, 0].set(b_expert)
    wr_hi = wr.astype(BF16)
    wr_lo = (wr - wr_hi.astype(F32)).astype(BF16)
    wr2 = jnp.concatenate([wr_hi, wr_lo], axis=0)

    def whole(shape):
        return pl.BlockSpec(shape, lambda i: (0,) * len(shape))

    return pl.pallas_call(
        _wo_ln_router_kernel,
        grid=(m // tm,),
        in_specs=[pl.BlockSpec((tm, d), lambda i: (i, 0)),
                  pl.BlockSpec((None, d, d), lambda i: (layer, 0, 0), pipeline_mode=pl.Buffered(1)),
                  pl.BlockSpec((tm, d), lambda i: (i, 0)),
                  whole((1, d)), whole((1, d)),
                  whole((2 * r, d)), whole((r, d)), whole((r, 1))],
        out_specs=[pl.BlockSpec((tm, d), lambda i: (i, 0)),
                   pl.BlockSpec((SUBLANES, tm), lambda i: (0, i)),
                   pl.BlockSpec((SUBLANES, tm), lambda i: (0, i))],
        out_shape=[jax.ShapeDtypeStruct((m, d), F32),
                   jax.ShapeDtypeStruct((SUBLANES, m), jnp.int32),
                   jax.ShapeDtypeStruct((SUBLANES, m), F32)],
        scratch_shapes=[pltpu.VMEM((d, d), BF16)],
        compiler_params=_params("arbitrary"),
        name="wo_ln_router",
    )(merged, w_o, x, ln_g.reshape(1, d), ln_b.reshape(1, d), wr2, wr_hi, br)


def _row_copy(src, src_row, dst, dst_row, sem):
    return pltpu.make_async_copy(src.at[pl.ds(src_row, 1)], dst.at[pl.ds(dst_row, 1)], sem)


def _rank_kernel(e_ref, pos_ref, cnt_ref, tri_ref, carry_ref):
    n = e_ref.shape[1]

    @pl.when(pl.program_id(0) == 0)
    def _():
        r = lax.broadcasted_iota(jnp.int32, (n, n), 0)
        c = lax.broadcasted_iota(jnp.int32, (n, n), 1)
        tri_ref[...] = jnp.where(r <= c, 1.0, 0.0).astype(BF16)
        carry_ref[...] = jnp.zeros_like(carry_ref)

    row = lax.broadcasted_iota(jnp.int32, (N_EXPERTS, n), 0)
    onehot = jnp.where(row == e_ref[...], 1.0, 0.0)
    prefix = jnp.dot(onehot.astype(BF16), tri_ref[...], preferred_element_type=F32)
    carry = carry_ref[...]
    pos = jnp.sum(onehot * (prefix + carry[:, 0:1]), axis=0, keepdims=True) - 1.0
    pos_ref[...] = pos.astype(jnp.int32)
    carry = carry + jnp.sum(onehot, axis=1, keepdims=True)
    carry_ref[...] = carry
    cnt_ref[...] = carry


def _rank(flat_e):
    n_asg = flat_e.shape[1]
    tile = min(512, n_asg)
    return pl.pallas_call(
        _rank_kernel,
        grid=(n_asg // tile,),
        in_specs=[pl.BlockSpec((1, tile), lambda i: (0, i))],
        out_specs=[pl.BlockSpec((1, tile), lambda i: (0, i)),
                   pl.BlockSpec((N_EXPERTS, LANES), lambda i: (0, 0))],
        out_shape=[jax.ShapeDtypeStruct((1, n_asg), jnp.int32),
                   jax.ShapeDtypeStruct((N_EXPERTS, LANES), F32)],
        scratch_shapes=[pltpu.VMEM((tile, tile), BF16), pltpu.VMEM((N_EXPERTS, LANES), F32)],
        compiler_params=_params("arbitrary"),
        name="rank",
    )(flat_e)


def _invert_kernel(d0_ref, d1_ref, zeros_hbm, o_ref, sem):
    clear = pltpu.make_async_copy(zeros_hbm, o_ref, sem)
    clear.start()
    clear.wait()

    def put(t, c):
        o_ref[d0_ref[t]] = t
        o_ref[d1_ref[t]] = t
        return c

    lax.fori_loop(0, d0_ref.shape[0], put, 0, unroll=8)


def _invert(dest0, dest1, n_pad):
    smem = pl.BlockSpec(memory_space=pltpu.SMEM)
    return pl.pallas_call(
        _invert_kernel,
        in_specs=[smem, smem, pl.BlockSpec(memory_space=pl.ANY)],
        out_specs=smem,
        out_shape=jax.ShapeDtypeStruct((n_pad,), jnp.int32),
        scratch_shapes=[pltpu.SemaphoreType.DMA(())],
        name="invert",
    )(dest0, dest1, jnp.zeros((n_pad,), jnp.int32))


GATHER_SLOTS = 3


def _expert_kernel(be_ref, nx_ref, nu_ref, rt_ref, x_hbm, wgu_hbm, wd_hbm, o_ref,
                   xbuf, wgu_s, wd_s, wgu_b, wd_b, gsem, wsem, *, rows, n_blocks, layer):
    i = pl.program_id(0)
    nu = nu_ref[0]
    ff = wd_b.shape[0]
    ns = GATHER_SLOTS

    def issue(blk, slot):
        base = blk * rows
        for r in range(rows):
            _row_copy(x_hbm, rt_ref[base + r], xbuf.at[slot], r, gsem.at[slot]).start()

    def drain(slot):
        for _ in range(rows):
            _row_copy(x_hbm, 0, xbuf.at[slot], 0, gsem.at[slot]).wait()

    def weight_copies(e):
        return (pltpu.make_async_copy(wgu_hbm.at[layer, e], wgu_s, wsem.at[0]),
                pltpu.make_async_copy(wd_hbm.at[layer, e], wd_s, wsem.at[1]))

    @pl.when(i == 0)
    def _():
        for cp in weight_copies(be_ref[0]):
            cp.start()
        issue(0, 0)
        issue(min(1, n_blocks - 1), 1)

    @pl.when((i < nu) & ((i == 0) | (be_ref[i] != be_ref[jnp.maximum(i - 1, 0)])))
    def _():
        for cp in weight_copies(be_ref[i]):
            cp.wait()
        wgu_b[...] = wgu_s[...].astype(BF16)
        wd_b[...] = wd_s[...].astype(BF16)

        @pl.when(nx_ref[i] >= 0)
        def _():
            for cp in weight_copies(nx_ref[i]):
                cp.start()

    def compute(slot):
        drain(slot)
        xb = xbuf[slot].astype(BF16)
        issue(jnp.minimum(i + 2, n_blocks - 1), (slot + 2) % ns)
        gu = jnp.dot(xb, wgu_b[...], preferred_element_type=F32)
        gate = gu[:, :ff]
        hmid = gate * jax.nn.sigmoid(gate) * gu[:, ff:]
        o_ref[...] = jnp.dot(hmid.astype(BF16), wd_b[...], preferred_element_type=F32)

    phase = lax.rem(i, ns)
    for slot in range(ns):
        @pl.when((i < nu) & (phase == slot))
        def _(slot=slot):
            compute(slot)

    for slot in range(ns):
        @pl.when((i == nu) & (phase == slot))
        def _(slot=slot):
            drain(slot)
            drain((slot + 1) % ns)

    @pl.when(i >= nu)
    def _():
        o_ref[...] = jnp.zeros_like(o_ref)


def _experts(x1, row_tok, block_e, next_e, n_used, w_gate_up, w_down, layer):
    d = x1.shape[1]
    n_pad = row_tok.shape[0]
    ff = w_down.shape[2]
    rows = EXPERT_ROWS
    n_blocks = n_pad // rows
    assert n_blocks >= 2
    return pl.pallas_call(
        functools.partial(_expert_kernel, rows=rows, n_blocks=n_blocks, layer=layer),
        grid_spec=pltpu.PrefetchScalarGridSpec(
            num_scalar_prefetch=4,
            grid=(n_blocks,),
            in_specs=[pl.BlockSpec(memory_space=pl.ANY),
                      pl.BlockSpec(memory_space=pl.ANY),
                      pl.BlockSpec(memory_space=pl.ANY)],
            out_specs=pl.BlockSpec((rows, d), lambda i, be, nx, nu, rt: (i, 0)),
            scratch_shapes=[pltpu.VMEM((GATHER_SLOTS, rows, d), F32),
                            pltpu.VMEM((d, 2 * ff), F32),
                            pltpu.VMEM((ff, d), F32),
                            pltpu.VMEM((d, 2 * ff), BF16),
                            pltpu.VMEM((ff, d), BF16),
                            pltpu.SemaphoreType.DMA((GATHER_SLOTS,)),
                            pltpu.SemaphoreType.DMA((2,))]),
        out_shape=jax.ShapeDtypeStruct((n_pad, d), F32),
        compiler_params=_params("arbitrary"),
        name="experts",
    )(block_e, next_e, n_used, row_tok, x1, w_gate_up, w_down)


def _combine_ln_kernel(d0_ref, d1_ref, y_hbm, w_ref, x_ref, g_ref, b_ref, o_ref, ob_ref, buf, sem,
                       *, tb, n_steps):
    i = pl.program_id(0)

    def issue(step, slot):
        base = step * tb
        for t in range(tb):
            _row_copy(y_hbm, d0_ref[base + t], buf.at[slot, 0], t, sem.at[slot]).start()
            _row_copy(y_hbm, d1_ref[base + t], buf.at[slot, 1], t, sem.at[slot]).start()

    def drain(slot):
        for _ in range(EXPERT_TOPK * tb):
            _row_copy(y_hbm, 0, buf.at[slot, 0], 0, sem.at[slot]).wait()

    @pl.when(i == 0)
    def _():
        issue(0, 0)

    def compute(slot):
        drain(slot)
        w = w_ref[...]
        ffn = w[:, 0:1] * buf[slot, 0] + w[:, 1:2] * buf[slot, 1]
        out = _layer_norm_rows(DEEPNORM_ALPHA * x_ref[...] + ffn, g_ref[...], b_ref[...])
        o_ref[...] = out
        ob_ref[...] = out.astype(BF16)

    for slot in range(2):
        @pl.when((i % 2 == slot) & (i + 1 < n_steps))
        def _(slot=slot):
            issue(i + 1, 1 - slot)

        @pl.when(i % 2 == slot)
        def _(slot=slot):
            compute(slot)


def _combine_ln(y_rows, dest0, dest1, w_tok, x1, ln_g, ln_b):
    m, d = x1.shape
    tb = min(128, m)
    n_steps = m // tb
    return pl.pallas_call(
        functools.partial(_combine_ln_kernel, tb=tb, n_steps=n_steps),
        grid_spec=pltpu.PrefetchScalarGridSpec(
            num_scalar_prefetch=2,
            grid=(n_steps,),
            in_specs=[pl.BlockSpec(memory_space=pl.ANY),
                      pl.BlockSpec((tb, EXPERT_TOPK), lambda i, a, b: (i, 0)),
                      pl.BlockSpec((tb, d), lambda i, a, b: (i, 0)),
                      pl.BlockSpec((1, d), lambda i, a, b: (0, 0)),
                      pl.BlockSpec((1, d), lambda i, a, b: (0, 0))],
            out_specs=[pl.BlockSpec((tb, d), lambda i, a, b: (i, 0)),
                       pl.BlockSpec((tb, d), lambda i, a, b: (i, 0))],
            scratch_shapes=[pltpu.VMEM((2, EXPERT_TOPK, tb, d), F32), pltpu.SemaphoreType.DMA((2,))]),
        out_shape=[jax.ShapeDtypeStruct((m, d), F32), jax.ShapeDtypeStruct((m, d), BF16)],
        compiler_params=_params("arbitrary"),
        name="combine_ln",
    )(dest0, dest1, y_rows, w_tok, x1, ln_g.reshape(1, d), ln_b.reshape(1, d))


def _slot_plan(ids, n_tok):
    rows = EXPERT_ROWS
    n_asg = n_tok * EXPERT_TOPK
    n_pad = n_asg + N_EXPERTS * rows
    flat_e = ids[:EXPERT_TOPK].reshape(1, n_asg)
    pos, cnt = _rank(flat_e)
    counts = cnt[:, 0].astype(jnp.int32)
    padded = ((counts + rows - 1) // rows) * rows
    pad_ends = jnp.cumsum(padded)
    pad_starts = pad_ends - padded
    experts = jnp.arange(N_EXPERTS, dtype=jnp.int32)
    start_of = jnp.sum(jnp.where(flat_e[0][:, None] == experts[None, :], pad_starts[None, :], 0), axis=1)
    dest = (start_of + pos[0]).astype(jnp.int32)
    n_blocks = n_pad // rows
    block_start = jnp.arange(n_blocks, dtype=jnp.int32) * rows
    block_e = jnp.minimum(jnp.sum((pad_ends[None, :] <= block_start[:, None]).astype(jnp.int32), axis=1),
                          N_EXPERTS - 1)
    later = (experts[None, :] > experts[:, None]) & (counts[None, :] > 0)
    next_of = jnp.min(jnp.where(later, experts[None, :], N_EXPERTS), axis=1)
    next_of = jnp.where(next_of == N_EXPERTS, -1, next_of)
    next_e = jnp.sum(jnp.where(block_e[:, None] == experts[None, :], next_of[None, :], 0), axis=1).astype(jnp.int32)
    n_used = (pad_ends[-1:] // rows).astype(jnp.int32)
    dest0, dest1 = dest[:n_tok], dest[n_tok:]
    return dest0, dest1, _invert(dest0, dest1, n_pad), block_e, next_e, n_used


def kernel(x, mem, w_in, p_moba, p_ret, p_mem, w_mem_kv, w_o, ln1_g, ln1_b, w_group, b_group,
           w_expert, b_expert, w_gate_up, w_down, ln2_g, ln2_b):
    bsz, seq, d = x.shape
    mem_len = mem.shape[1]
    n_tok = bsz * seq
    xf = x.reshape(n_tok, d)
    xb = xf.astype(BF16)
    mem_b = mem.reshape(bsz * mem_len, d).astype(BF16)
    for l in range(w_in.shape[0]):
        proj = _matmul(xb, w_in, l, BF16, 1024, 1024)
        kvm = _matmul(mem_b, w_mem_kv, l, BF16, 512, 1024)
        y_a = _moba(proj, bsz, seq)
        y_r = _retention(proj, bsz, seq)
        y_m = _mem_attention(proj, kvm, bsz, seq, mem_len)
        merged = _merge(y_a, y_r, y_m, p_moba, p_ret, p_mem, l, proj)
        x1, ids, wts = _wo_ln_router(merged, w_o, l, xf, ln1_g[l], ln1_b[l],
                                     w_group[l], b_group[l], w_expert[l], b_expert[l])
        dest0, dest1, row_tok, block_e, next_e, n_used = _slot_plan(ids, n_tok)
        y_rows = _experts(x1, row_tok, block_e, next_e, n_used, w_gate_up, w_down, l)
        xf, xb = _combine_ln(y_rows, dest0, dest1, wts[:EXPERT_TOPK].T, x1, ln2_g[l], ln2_b[l])
    return xf.reshape(bsz, seq, d)
```

```python
import functools

import jax
import jax.numpy as jnp
from jax import lax
from jax.experimental import pallas as pl
from jax.experimental.pallas import tpu as pltpu

F32 = jnp.float32
BF16 = jnp.bfloat16

D_MODEL = 2048
DEPTH = 2
MOBA_HEADS = 8
MOBA_HEAD_DIM = 128
MOBA_BLOCK = 256
MOBA_TOPK = 3
RET_HEADS = 4
RET_KEY_DIM = 256
RET_VALUE_DIM = 512
RET_CHUNK = 128
ROPE_BASE = 10000.0
MEM_HEADS = 4
MEM_HEAD_DIM = 256
N_BRANCHES = 3
MOBA_WIDTH = MOBA_HEADS * MOBA_HEAD_DIM
RET_QK_WIDTH = RET_HEADS * RET_KEY_DIM
RET_V_WIDTH = RET_HEADS * RET_VALUE_DIM
MEM_WIDTH = MEM_HEADS * MEM_HEAD_DIM
N_GROUPS = 4
EXPERTS_PER_GROUP = 8
N_EXPERTS = N_GROUPS * EXPERTS_PER_GROUP
EXPERT_TOPK = 2
EXPERT_FF = 512
DEEPNORM_ALPHA = (2 * DEPTH) ** 0.25
LN_EPS = 1e-5
GN_EPS = 1e-6
NEG_INF = -1e30
LOG2_E = 1.4426950408889634

OFF_AQ = 0
OFF_AK = OFF_AQ + MOBA_WIDTH
OFF_AV = OFF_AK + MOBA_WIDTH
OFF_RQ = OFF_AV + MOBA_WIDTH
OFF_RK = OFF_RQ + RET_QK_WIDTH
OFF_RV = OFF_RK + RET_QK_WIDTH
OFF_RG = OFF_RV + RET_V_WIDTH
OFF_MQ = OFF_RG + RET_V_WIDTH
OFF_GA = OFF_MQ + MEM_WIDTH
OFF_GR = OFF_GA + D_MODEL
OFF_GM = OFF_GR + D_MODEL

LANES = 128
SUBLANES = 8
VMEM_LIMIT = 56 * 1024 * 1024

EXPERT_ROWS = 256
ROUTER_ROWS = 48
ROW_TILE = 256

_NT = (((1,), (1,)), ((), ()))


def _params(*sem):
    return pltpu.CompilerParams(dimension_semantics=sem, vmem_limit_bytes=VMEM_LIMIT)


def _mm_kernel(a_ref, w_ref, o_ref, wb_ref):
    @pl.when(pl.program_id(1) == 0)
    def _():
        wb_ref[...] = w_ref[...].astype(BF16)

    o_ref[...] = jnp.dot(a_ref[...], wb_ref[...], preferred_element_type=F32).astype(o_ref.dtype)


def _matmul(a, w, layer, out_dtype, tm, tn):
    m, k = a.shape
    n = w.shape[2]
    tm, tn = min(tm, m), min(tn, n)
    return pl.pallas_call(
        _mm_kernel,
        grid=(n // tn, m // tm),
        in_specs=[pl.BlockSpec((tm, k), lambda j, i: (i, 0)),
                  pl.BlockSpec((None, k, tn), lambda j, i: (layer, 0, j))],
        out_specs=pl.BlockSpec((tm, tn), lambda j, i: (i, j)),
        out_shape=jax.ShapeDtypeStruct((m, n), out_dtype),
        scratch_shapes=[pltpu.VMEM((k, tn), BF16)],
        compiler_params=_params("parallel", "arbitrary"),
        name="matmul",
    )(a, w)


def _moba_kernel(q_ref, k_ref, v_ref, o_ref, vt_ref, *, n_blk, blk, top_n, scale):
    seq = n_blk * blk
    dh = q_ref.shape[1]
    shift = blk.bit_length() - 1

    row = lax.broadcasted_iota(jnp.int32, (2 * SUBLANES, seq), 0)
    col_blk = lax.shift_right_logical(lax.broadcasted_iota(jnp.int32, (2 * SUBLANES, seq), 1), shift)
    ind = jnp.where(col_blk == (row & (SUBLANES - 1)), 1.0, 0.0).astype(BF16)
    kmean = jnp.dot(ind, k_ref[...], preferred_element_type=F32) * (1.0 / blk)
    hi = kmean.astype(BF16).astype(F32)
    r2 = lax.broadcasted_iota(jnp.int32, (2 * SUBLANES, dh), 0)
    km = jnp.where(r2 < SUBLANES, hi, kmean - hi).astype(BF16)
    for j in range(n_blk):
        vt_ref[:, j * blk:(j + 1) * blk] = v_ref[j * blk:(j + 1) * blk, :].astype(F32).T.astype(BF16)

    g16 = lax.dot_general(km, q_ref[...], _NT, preferred_element_type=F32)
    g = g16[0:SUBLANES] + g16[SUBLANES:2 * SUBLANES]
    jrow = lax.broadcasted_iota(jnp.int32, (SUBLANES, seq), 0)
    q_blk = lax.shift_right_logical(lax.broadcasted_iota(jnp.int32, (SUBLANES, seq), 1), shift)
    past = jrow < q_blk
    g = jnp.where(past, g, NEG_INF)
    rank = jnp.zeros((SUBLANES, seq), F32)
    for jp in range(SUBLANES):
        gj = g[jp:jp + 1, :]
        rank = rank + jnp.where(gj > g, 1.0, jnp.where((gj == g) & (jp < jrow), 1.0, 0.0))
    sel = jnp.where(past & (rank < top_n), 1.0, 0.0)

    causal = (lax.broadcasted_iota(jnp.int32, (blk, blk), 0)
              <= lax.broadcasted_iota(jnp.int32, (blk, blk), 1))
    def scores(i):
        return lax.dot_general(k_ref[0:(i + 1) * blk, :], q_ref[i * blk:(i + 1) * blk, :], _NT,
                               preferred_element_type=F32)

    def emit(i, p_all, denom):
        acc = jnp.dot(vt_ref[:, 0:(i + 1) * blk], p_all, preferred_element_type=F32)
        o_ref[i * blk:(i + 1) * blk, :] = (acc * (1.0 / denom)).T.astype(o_ref.dtype)

    s_next = scores(0)
    pending = None
    for i in range(n_blk):
        lo, hi_q = i * blk, (i + 1) * blk
        s = s_next
        if i + 1 < n_blk:
            s_next = scores(i + 1)
        parts = [jnp.where(sel[j:j + 1, lo:hi_q] > 0.5, s[j * blk:(j + 1) * blk], NEG_INF) for j in range(i)]
        parts.append(jnp.where(causal, s[lo:hi_q], NEG_INF))
        m = functools.reduce(jnp.maximum, [jnp.max(p, axis=0, keepdims=True) for p in parts])
        probs = [jnp.exp2((p - m) * (scale * LOG2_E)) for p in parts]
        denom = functools.reduce(jnp.add, [jnp.sum(p, axis=0, keepdims=True) for p in probs])
        p_all = jnp.concatenate([p.astype(BF16) for p in probs], axis=0)
        if pending is not None:
            emit(*pending)
        pending = (i, p_all, denom)
    emit(*pending)


def _moba(proj, bsz, seq):
    n_blk = seq // MOBA_BLOCK
    assert seq % MOBA_BLOCK == 0 and n_blk <= SUBLANES
    dh = MOBA_HEAD_DIM
    kern = functools.partial(_moba_kernel, n_blk=n_blk, blk=MOBA_BLOCK, top_n=min(MOBA_TOPK, n_blk),
                             scale=dh ** -0.5)
    return pl.pallas_call(
        kern,
        grid=(bsz, MOBA_HEADS),
        in_specs=[pl.BlockSpec((seq, dh), lambda b, h: (b, OFF_AQ // dh + h)),
                  pl.BlockSpec((seq, dh), lambda b, h: (b, OFF_AK // dh + h)),
                  pl.BlockSpec((seq, dh), lambda b, h: (b, OFF_AV // dh + h))],
        out_specs=pl.BlockSpec((seq, dh), lambda b, h: (b, h)),
        out_shape=jax.ShapeDtypeStruct((bsz * seq, MOBA_WIDTH), BF16),
        scratch_shapes=[pltpu.VMEM((dh, seq), BF16)],
        compiler_params=_params("parallel", "parallel"),
        name="moba",
    )(proj, proj, proj)


def _ret_kernel(cd_ref, q_ref, k_ref, v_ref, g_ref, cos_ref, sin_ref, dec_ref, zeta_ref, xi_ref,
                o_ref, st_ref, *, dk, c):
    cd = cd_ref[pl.program_id(1)]
    n_c = q_ref.shape[0] // c
    half = dk // 2

    def widen(t):
        return jnp.concatenate([t] * (dk // LANES), axis=1)

    dec = dec_ref[...]
    xi = widen(xi_ref[...])
    zeta = widen(zeta_ref[...])

    def within_chunk(n):
        r = slice(n * c, (n + 1) * c)
        cos = cos_ref[r, :]
        sin = sin_ref[r, :]

        def rot(x):
            x1 = x[:, :half]
            x2 = x[:, half:]
            return jnp.concatenate([x1 * cos - x2 * sin, x1 * sin + x2 * cos], axis=1)

        q = rot(q_ref[r, :].astype(F32))
        k = rot(k_ref[r, :].astype(F32)) * (dk ** -0.5)
        v = v_ref[r, :]
        sc = lax.dot_general(q.astype(BF16), k.astype(BF16), _NT, preferred_element_type=F32) * dec
        inner = jnp.dot(sc.astype(BF16), v, preferred_element_type=F32)
        kv = jnp.dot((k * zeta).T.astype(BF16), v, preferred_element_type=F32) if n < n_c - 1 else None
        return inner, kv, (q * xi).astype(BF16)

    ahead = within_chunk(0)
    for n in range(n_c):
        r = slice(n * c, (n + 1) * c)
        o, kv, qx = ahead
        if n + 1 < n_c:
            ahead = within_chunk(n + 1)
        if n > 0:
            o = o + jnp.dot(qx, st_ref[...].astype(BF16), preferred_element_type=F32)
        if kv is not None:
            st_ref[...] = kv if n == 0 else cd * st_ref[...] + kv

        mu = jnp.mean(o, axis=1, keepdims=True)
        d = o - mu
        var = jnp.mean(d * d, axis=1, keepdims=True)
        gg = g_ref[r, :].astype(F32)
        o_ref[r, :] = (gg * jax.nn.sigmoid(gg) * (d * lax.rsqrt(var + GN_EPS))).astype(o_ref.dtype)


def _retention(proj, bsz, seq):
    c, dk, dv, nh = RET_CHUNK, RET_KEY_DIM, RET_VALUE_DIM, RET_HEADS
    n_c = seq // c
    half = dk // 2
    pos = jnp.arange(seq, dtype=F32)
    inv = ROPE_BASE ** (-jnp.linspace(0.0, 1.0, half, dtype=F32))
    ang = pos[:, None] * inv[None, :]
    cos, sin = jnp.cos(ang), jnp.sin(ang)
    log_g = jnp.log1p(-jnp.power(2.0, -5.0 - jnp.arange(nh, dtype=F32)))
    idx = jnp.arange(c, dtype=F32)
    diff = idx[:, None] - idx[None, :]
    decay = jnp.where(diff >= 0, jnp.exp(log_g[:, None, None] * jnp.maximum(diff, 0.0)), 0.0)
    zeta = jnp.exp(log_g[:, None] * (c - 1 - idx)[None, :])
    xi = jnp.exp(log_g[:, None] * (idx + 1.0)[None, :])
    zeta_b = jnp.broadcast_to(zeta[:, :, None], (nh, c, LANES))
    xi_b = jnp.broadcast_to(xi[:, :, None], (nh, c, LANES))
    chunk_decay = jnp.exp(log_g * c)

    return pl.pallas_call(
        functools.partial(_ret_kernel, dk=dk, c=c),
        grid=(bsz, nh),
        in_specs=[pl.BlockSpec(memory_space=pltpu.SMEM),
                  pl.BlockSpec((seq, dk), lambda b, h: (b, OFF_RQ // dk + h)),
                  pl.BlockSpec((seq, dk), lambda b, h: (b, OFF_RK // dk + h)),
                  pl.BlockSpec((seq, dv), lambda b, h: (b, OFF_RV // dv + h)),
                  pl.BlockSpec((seq, dv), lambda b, h: (b, OFF_RG // dv + h)),
                  pl.BlockSpec((seq, half), lambda b, h: (0, 0)),
                  pl.BlockSpec((seq, half), lambda b, h: (0, 0)),
                  pl.BlockSpec((None, c, c), lambda b, h: (h, 0, 0)),
                  pl.BlockSpec((None, c, LANES), lambda b, h: (h, 0, 0)),
                  pl.BlockSpec((None, c, LANES), lambda b, h: (h, 0, 0))],
        out_specs=pl.BlockSpec((seq, dv), lambda b, h: (b, h)),
        out_shape=jax.ShapeDtypeStruct((bsz * seq, RET_V_WIDTH), BF16),
        scratch_shapes=[pltpu.VMEM((dk, dv), F32)],
        compiler_params=_params("parallel", "parallel"),
        name="retention",
    )(chunk_decay, proj, proj, proj, proj, cos, sin, decay, zeta_b, xi_b)


def _mem_kernel(q_ref, k_ref, v_ref, o_ref, *, scale):
    s = lax.dot_general(q_ref[...], k_ref[...], _NT, preferred_element_type=F32) * scale
    m = jnp.max(s, axis=1, keepdims=True)
    p = jnp.exp(s - m)
    p = p * (1.0 / jnp.sum(p, axis=1, keepdims=True))
    o_ref[...] = jnp.dot(p.astype(BF16), v_ref[...], preferred_element_type=F32).astype(o_ref.dtype)


def _mem_attention(proj, kvm, bsz, seq, mem_len):
    dh, nh = MEM_HEAD_DIM, MEM_HEADS
    tq = min(2048, seq)
    n_q = seq // tq
    return pl.pallas_call(
        functools.partial(_mem_kernel, scale=dh ** -0.5),
        grid=(bsz, nh, n_q),
        in_specs=[pl.BlockSpec((tq, dh), lambda b, h, i: (b * n_q + i, OFF_MQ // dh + h)),
                  pl.BlockSpec((mem_len, dh), lambda b, h, i: (b, h)),
                  pl.BlockSpec((mem_len, dh), lambda b, h, i: (b, nh + h))],
        out_specs=pl.BlockSpec((tq, dh), lambda b, h, i: (b * n_q + i, h)),
        out_shape=jax.ShapeDtypeStruct((bsz * seq, MEM_WIDTH), BF16),
        compiler_params=_params("parallel", "parallel", "parallel"),
        name="mem_attention",
    )(proj, kvm, kvm)


def _merge_kernel(ya_ref, yr_ref, ym_ref, pa_ref, pr_ref, pm_ref, ga_ref, gr_ref, gm_ref, o_ref,
                  pab_ref, prb_ref, pmb_ref):
    @pl.when(pl.program_id(1) == 0)
    def _():
        pab_ref[...] = pa_ref[...].astype(BF16)
        prb_ref[...] = pr_ref[...].astype(BF16)
        pmb_ref[...] = pm_ref[...].astype(BF16)

    def branch(y_ref, p_ref, gate_ref):
        t = jnp.dot(y_ref[...], p_ref[...], preferred_element_type=F32)
        return jax.nn.sigmoid(gate_ref[...].astype(F32)) * t

    o_ref[...] = (branch(ya_ref, pab_ref, ga_ref) + branch(yr_ref, prb_ref, gr_ref)
                  + branch(ym_ref, pmb_ref, gm_ref)).astype(o_ref.dtype)


def _merge(y_a, y_r, y_m, p_moba, p_ret, p_mem, layer, proj):
    m = y_a.shape[0]
    tm, tn = min(512, m), 1024
    d = D_MODEL

    def rows(width):
        return pl.BlockSpec((tm, width), lambda j, i: (i, 0))

    def wcols(kdim):
        return pl.BlockSpec((None, kdim, tn), lambda j, i: (layer, 0, j), pipeline_mode=pl.Buffered(1))

    def gate(off):
        return pl.BlockSpec((tm, tn), lambda j, i: (i, off // tn + j))

    return pl.pallas_call(
        _merge_kernel,
        grid=(d // tn, m // tm),
        in_specs=[rows(MOBA_WIDTH), rows(RET_V_WIDTH), rows(MEM_WIDTH),
                  wcols(MOBA_WIDTH), wcols(RET_V_WIDTH), wcols(MEM_WIDTH),
                  gate(OFF_GA), gate(OFF_GR), gate(OFF_GM)],
        out_specs=pl.BlockSpec((tm, tn), lambda j, i: (i, j)),
        out_shape=jax.ShapeDtypeStruct((m, d), BF16),
        scratch_shapes=[pltpu.VMEM((MOBA_WIDTH, tn), BF16), pltpu.VMEM((RET_V_WIDTH, tn), BF16),
                        pltpu.VMEM((MEM_WIDTH, tn), BF16)],
        compiler_params=_params("parallel", "arbitrary"),
        name="merge",
    )(y_a, y_r, y_m, p_moba, p_ret, p_mem, proj, proj, proj)


def _layer_norm_rows(z, g, b):
    mu = jnp.mean(z, axis=1, keepdims=True)
    d = z - mu
    var = jnp.mean(d * d, axis=1, keepdims=True)
    return d * lax.rsqrt(var + LN_EPS) * g + b


def _wo_ln_router_kernel(mg_ref, wo_hbm, x_ref, g_ref, b_ref, wr2_ref, wr1_ref, br_ref,
                         x1_ref, ids_ref, wts_ref, wob_ref, stage_ref, sem, *, layer, sub):
    @pl.when(pl.program_id(0) == 0)
    def _():
        chunk = stage_ref.shape[0]
        for c in range(wob_ref.shape[0] // chunk):
            cp = pltpu.make_async_copy(wo_hbm.at[layer, pl.ds(c * chunk, chunk)], stage_ref, sem)
            cp.start()
            cp.wait()
            wob_ref[c * chunk:(c + 1) * chunk, :] = stage_ref[...].astype(BF16)

    n_sub = mg_ref.shape[0] // sub

    def mix_of(h):
        return jnp.dot(mg_ref[h * sub:(h + 1) * sub, :], wob_ref[...], preferred_element_type=F32)

    mix_next = mix_of(0)
    for h in range(n_sub):
        rows = slice(h * sub, (h + 1) * sub)
        mix = mix_next
        if h + 1 < n_sub:
            mix_next = mix_of(h + 1)
        _ln_router_rows(mix, x_ref[rows, :], g_ref, b_ref, wr2_ref, wr1_ref, br_ref,
                        x1_ref.at[rows, :], ids_ref.at[:, rows], wts_ref.at[:, rows])


def _ln_router_rows(mix, x, g_ref, b_ref, wr2_ref, wr1_ref, br_ref, x1_ref, ids_ref, wts_ref):
    x1 = _layer_norm_rows(DEEPNORM_ALPHA * x + mix, g_ref[...], b_ref[...])
    x1_ref[...] = x1

    r = ROUTER_ROWS
    x_hi = x1.astype(BF16)
    x_lo = (x1 - x_hi.astype(F32)).astype(BF16)
    l2 = lax.dot_general(wr2_ref[...], x_hi, _NT, preferred_element_type=F32)
    l1 = lax.dot_general(wr1_ref[...], x_lo, _NT, preferred_element_type=F32)
    logit = l2[0:r] + l2[r:2 * r] + l1 + br_ref[...]

    tok = logit.shape[1]
    row = lax.broadcasted_iota(jnp.int32, (SUBLANES, tok), 0).astype(F32)
    gl = jnp.where(row < N_GROUPS, logit[0:SUBLANES], NEG_INF)
    g_max = jnp.max(gl, axis=0, keepdims=True)
    g_sel = jnp.min(jnp.where(gl == g_max, row, float(SUBLANES)), axis=0, keepdims=True)
    p_g = 1.0 / jnp.sum(jnp.exp(gl - g_max), axis=0, keepdims=True)
    e_in = jnp.zeros((SUBLANES, tok), F32)
    for grp in range(N_GROUPS):
        lo = SUBLANES * (1 + grp)
        e_in = jnp.where(g_sel == float(grp), logit[lo:lo + EXPERTS_PER_GROUP], e_in)
    v1 = jnp.max(e_in, axis=0, keepdims=True)
    i1 = jnp.min(jnp.where(e_in == v1, row, float(SUBLANES)), axis=0, keepdims=True)
    e_rest = jnp.where(row == i1, -jnp.inf, e_in)
    v2 = jnp.max(e_rest, axis=0, keepdims=True)
    i2 = jnp.min(jnp.where(e_rest == v2, row, float(SUBLANES)), axis=0, keepdims=True)
    t = jnp.exp(v2 - v1)
    w1 = p_g / (1.0 + t)
    w2 = w1 * t
    e1 = g_sel * float(EXPERTS_PER_GROUP) + i1
    e2 = g_sel * float(EXPERTS_PER_GROUP) + i2
    ids_ref[...] = jnp.where(row == 0.0, e1, jnp.where(row == 1.0, e2, 0.0)).astype(jnp.int32)
    wts_ref[...] = jnp.where(row == 0.0, w1, jnp.where(row == 1.0, w2, 0.0))


def _wo_ln_router(merged, w_o, layer, x, ln_g, ln_b, w_group, b_group, w_expert, b_expert):
    m, d = x.shape
    sub = min(ROW_TILE, m)
    tm = min(2 * ROW_TILE, m)
    r = ROUTER_ROWS
    wr = jnp.zeros((r, d), F32).at[0:N_GROUPS].set(w_group.T).at[SUBLANES:SUBLANES + N_EXPERTS].set(w_expert.T)
    br = jnp.zeros((r, 1), F32).at[0:N_GROUPS, 0].set(b_group).at[SUBLANES:SUBLANES + N_EXPERTS, 0].set(b_expert)
    wr_hi = wr.astype(BF16)
    wr_lo = (wr - wr_hi.astype(F32)).astype(BF16)
    wr2 = jnp.concatenate([wr_hi, wr_lo], axis=0)

    def whole(shape):
        return pl.BlockSpec(shape, lambda i: (0,) * len(shape))

    return pl.pallas_call(
        functools.partial(_wo_ln_router_kernel, layer=layer, sub=sub),
        grid=(m // tm,),
        in_specs=[pl.BlockSpec((tm, d), lambda i: (i, 0)),
                  pl.BlockSpec(memory_space=pl.ANY),
                  pl.BlockSpec((tm, d), lambda i: (i, 0)),
                  whole((1, d)), whole((1, d)),
                  whole((2 * r, d)), whole((r, d)), whole((r, 1))],
        out_specs=[pl.BlockSpec((tm, d), lambda i: (i, 0)),
                   pl.BlockSpec((SUBLANES, tm), lambda i: (0, i)),
                   pl.BlockSpec((SUBLANES, tm), lambda i: (0, i))],
        out_shape=[jax.ShapeDtypeStruct((m, d), F32),
                   jax.ShapeDtypeStruct((SUBLANES, m), jnp.int32),
                   jax.ShapeDtypeStruct((SUBLANES, m), F32)],
        scratch_shapes=[pltpu.VMEM((d, d), BF16), pltpu.VMEM((d // 4, d), F32), pltpu.SemaphoreType.DMA(())],
        compiler_params=_params("arbitrary"),
        name="wo_ln_router",
    )(merged, w_o, x, ln_g.reshape(1, d), ln_b.reshape(1, d), wr2, wr_hi, br)


def _row_copy(src, src_row, dst, dst_row, sem):
    return pltpu.make_async_copy(src.at[pl.ds(src_row, 1)], dst.at[pl.ds(dst_row, 1)], sem)


def _rank_kernel(e_ref, pos_ref, cnt_ref, tri_ref, carry_ref):
    n = e_ref.shape[1]

    @pl.when(pl.program_id(0) == 0)
    def _():
        r = lax.broadcasted_iota(jnp.int32, (n, n), 0)
        c = lax.broadcasted_iota(jnp.int32, (n, n), 1)
        tri_ref[...] = jnp.where(r <= c, 1.0, 0.0).astype(BF16)
        carry_ref[...] = jnp.zeros_like(carry_ref)

    row = lax.broadcasted_iota(jnp.int32, (N_EXPERTS, n), 0)
    onehot = jnp.where(row == e_ref[...], 1.0, 0.0)
    prefix = jnp.dot(onehot.astype(BF16), tri_ref[...], preferred_element_type=F32)
    carry = carry_ref[...]
    pos = jnp.sum(onehot * (prefix + carry[:, 0:1]), axis=0, keepdims=True) - 1.0
    pos_ref[...] = pos.astype(jnp.int32)
    carry = carry + jnp.sum(onehot, axis=1, keepdims=True)
    carry_ref[...] = carry
    cnt_ref[...] = carry


def _rank(flat_e):
    n_asg = flat_e.shape[1]
    tile = min(512, n_asg)
    return pl.pallas_call(
        _rank_kernel,
        grid=(n_asg // tile,),
        in_specs=[pl.BlockSpec((1, tile), lambda i: (0, i))],
        out_specs=[pl.BlockSpec((1, tile), lambda i: (0, i)),
                   pl.BlockSpec((N_EXPERTS, LANES), lambda i: (0, 0))],
        out_shape=[jax.ShapeDtypeStruct((1, n_asg), jnp.int32),
                   jax.ShapeDtypeStruct((N_EXPERTS, LANES), F32)],
        scratch_shapes=[pltpu.VMEM((tile, tile), BF16), pltpu.VMEM((N_EXPERTS, LANES), F32)],
        compiler_params=_params("arbitrary"),
        name="rank",
    )(flat_e)


def _invert_kernel(d0_ref, d1_ref, zeros_hbm, o_ref, sem):
    clear = pltpu.make_async_copy(zeros_hbm, o_ref, sem)
    clear.start()
    clear.wait()

    def put(t, c):
        o_ref[d0_ref[t]] = t
        o_ref[d1_ref[t]] = t
        return c

    lax.fori_loop(0, d0_ref.shape[0], put, 0, unroll=8)


def _invert(dest0, dest1, n_pad):
    smem = pl.BlockSpec(memory_space=pltpu.SMEM)
    return pl.pallas_call(
        _invert_kernel,
        in_specs=[smem, smem, pl.BlockSpec(memory_space=pl.ANY)],
        out_specs=smem,
        out_shape=jax.ShapeDtypeStruct((n_pad,), jnp.int32),
        scratch_shapes=[pltpu.SemaphoreType.DMA(())],
        name="invert",
    )(dest0, dest1, jnp.zeros((n_pad,), jnp.int32))


GATHER_SLOTS = 3


def _expert_kernel(be_ref, nx_ref, nu_ref, rt_ref, x_hbm, wgu_hbm, wd_hbm, o_ref,
                   xbuf, wgu_s, wd_s, wgu_b, wd_b, gsem, wsem, *, rows, n_blocks, layer):
    i = pl.program_id(0)
    nu = nu_ref[0]
    ff = wd_b.shape[0]
    ns = GATHER_SLOTS

    def issue(blk, slot):
        base = blk * rows
        for r in range(rows):
            _row_copy(x_hbm, rt_ref[base + r], xbuf.at[slot], r, gsem.at[slot]).start()

    def drain(slot):
        for _ in range(rows):
            _row_copy(x_hbm, 0, xbuf.at[slot], 0, gsem.at[slot]).wait()

    def weight_copies(e):
        return (pltpu.make_async_copy(wgu_hbm.at[layer, e], wgu_s, wsem.at[0]),
                pltpu.make_async_copy(wd_hbm.at[layer, e], wd_s, wsem.at[1]))

    @pl.when(i == 0)
    def _():
        for cp in weight_copies(be_ref[0]):
            cp.start()
        issue(0, 0)
        issue(min(1, n_blocks - 1), 1)

    @pl.when((i < nu) & ((i == 0) | (be_ref[i] != be_ref[jnp.maximum(i - 1, 0)])))
    def _():
        for cp in weight_copies(be_ref[i]):
            cp.wait()
        wgu_b[...] = wgu_s[...].astype(BF16)
        wd_b[...] = wd_s[...].astype(BF16)

        @pl.when(nx_ref[i] >= 0)
        def _():
            for cp in weight_copies(nx_ref[i]):
                cp.start()

    def compute(slot):
        drain(slot)
        xb = xbuf[slot].astype(BF16)
        issue(jnp.minimum(i + 2, n_blocks - 1), (slot + 2) % ns)
        gu = jnp.dot(xb, wgu_b[...], preferred_element_type=F32)
        gate = gu[:, :ff]
        hmid = gate * jax.nn.sigmoid(gate) * gu[:, ff:]
        o_ref[...] = jnp.dot(hmid.astype(BF16), wd_b[...], preferred_element_type=F32)

    phase = lax.rem(i, ns)
    for slot in range(ns):
        @pl.when((i < nu) & (phase == slot))
        def _(slot=slot):
            compute(slot)

    for slot in range(ns):
        @pl.when((i == nu) & (phase == slot))
        def _(slot=slot):
            drain(slot)
            drain((slot + 1) % ns)

    @pl.when(i >= nu)
    def _():
        o_ref[...] = jnp.zeros_like(o_ref)


def _experts(x1, row_tok, block_e, next_e, n_used, w_gate_up, w_down, layer):
    d = x1.shape[1]
    n_pad = row_tok.shape[0]
    ff = w_down.shape[2]
    rows = EXPERT_ROWS
    n_blocks = n_pad // rows
    assert n_blocks >= 2
    return pl.pallas_call(
        functools.partial(_expert_kernel, rows=rows, n_blocks=n_blocks, layer=layer),
        grid_spec=pltpu.PrefetchScalarGridSpec(
            num_scalar_prefetch=4,
            grid=(n_blocks,),
            in_specs=[pl.BlockSpec(memory_space=pl.ANY),
                      pl.BlockSpec(memory_space=pl.ANY),
                      pl.BlockSpec(memory_space=pl.ANY)],
            out_specs=pl.BlockSpec((rows, d), lambda i, be, nx, nu, rt: (i, 0)),
            scratch_shapes=[pltpu.VMEM((GATHER_SLOTS, rows, d), F32),
                            pltpu.VMEM((d, 2 * ff), F32),
                            pltpu.VMEM((ff, d), F32),
                            pltpu.VMEM((d, 2 * ff), BF16),
                            pltpu.VMEM((ff, d), BF16),
                            pltpu.SemaphoreType.DMA((GATHER_SLOTS,)),
                            pltpu.SemaphoreType.DMA((2,))]),
        out_shape=jax.ShapeDtypeStruct((n_pad, d), F32),
        compiler_params=_params("arbitrary"),
        name="experts",
    )(block_e, next_e, n_used, row_tok, x1, w_gate_up, w_down)


def _combine_ln_kernel(d0_ref, d1_ref, y_hbm, w_ref, x_ref, g_ref, b_ref, o_ref, ob_ref, buf, sem,
                       *, tb, n_steps):
    i = pl.program_id(0)

    def issue(step, slot):
        base = step * tb
        for t in range(tb):
            _row_copy(y_hbm, d0_ref[base + t], buf.at[slot, 0], t, sem.at[slot]).start()
            _row_copy(y_hbm, d1_ref[base + t], buf.at[slot, 1], t, sem.at[slot]).start()

    def drain(slot):
        for _ in range(EXPERT_TOPK * tb):
            _row_copy(y_hbm, 0, buf.at[slot, 0], 0, sem.at[slot]).wait()

    @pl.when(i == 0)
    def _():
        issue(0, 0)

    def compute(slot):
        drain(slot)
        w = w_ref[...]
        ffn = w[:, 0:1] * buf[slot, 0] + w[:, 1:2] * buf[slot, 1]
        out = _layer_norm_rows(DEEPNORM_ALPHA * x_ref[...] + ffn, g_ref[...], b_ref[...])
        o_ref[...] = out
        ob_ref[...] = out.astype(BF16)

    for slot in range(2):
        @pl.when((i % 2 == slot) & (i + 1 < n_steps))
        def _(slot=slot):
            issue(i + 1, 1 - slot)

        @pl.when(i % 2 == slot)
        def _(slot=slot):
            compute(slot)


def _combine_ln(y_rows, dest0, dest1, w_tok, x1, ln_g, ln_b):
    m, d = x1.shape
    tb = min(128, m)
    n_steps = m // tb
    return pl.pallas_call(
        functools.partial(_combine_ln_kernel, tb=tb, n_steps=n_steps),
        grid_spec=pltpu.PrefetchScalarGridSpec(
            num_scalar_prefetch=2,
            grid=(n_steps,),
            in_specs=[pl.BlockSpec(memory_space=pl.ANY),
                      pl.BlockSpec((tb, EXPERT_TOPK), lambda i, a, b: (i, 0)),
                      pl.BlockSpec((tb, d), lambda i, a, b: (i, 0)),
                      pl.BlockSpec((1, d), lambda i, a, b: (0, 0)),
                      pl.BlockSpec((1, d), lambda i, a, b: (0, 0))],
            out_specs=[pl.BlockSpec((tb, d), lambda i, a, b: (i, 0)),
                       pl.BlockSpec((tb, d), lambda i, a, b: (i, 0))],
            scratch_shapes=[pltpu.VMEM((2, EXPERT_TOPK, tb, d), F32), pltpu.SemaphoreType.DMA((2,))]),
        out_shape=[jax.ShapeDtypeStruct((m, d), F32), jax.ShapeDtypeStruct((m, d), BF16)],
        compiler_params=_params("arbitrary"),
        name="combine_ln",
    )(dest0, dest1, y_rows, w_tok, x1, ln_g.reshape(1, d), ln_b.reshape(1, d))


def _slot_plan(ids, n_tok):
    rows = EXPERT_ROWS
    n_asg = n_tok * EXPERT_TOPK
    n_pad = n_asg + N_EXPERTS * rows
    flat_e = ids[:EXPERT_TOPK].reshape(1, n_asg)
    pos, cnt = _rank(flat_e)
    counts = cnt[:, 0].astype(jnp.int32)
    padded = ((counts + rows - 1) // rows) * rows
    pad_ends = jnp.cumsum(padded)
    pad_starts = pad_ends - padded
    experts = jnp.arange(N_EXPERTS, dtype=jnp.int32)
    start_of = jnp.sum(jnp.where(flat_e[0][:, None] == experts[None, :], pad_starts[None, :], 0), axis=1)
    dest = (start_of + pos[0]).astype(jnp.int32)
    n_blocks = n_pad // rows
    block_start = jnp.arange(n_blocks, dtype=jnp.int32) * rows
    block_e = jnp.minimum(jnp.sum((pad_ends[None, :] <= block_start[:, None]).astype(jnp.int32), axis=1),
                          N_EXPERTS - 1)
    later = (experts[None, :] > experts[:, None]) & (counts[None, :] > 0)
    next_of = jnp.min(jnp.where(later, experts[None, :], N_EXPERTS), axis=1)
    next_of = jnp.where(next_of == N_EXPERTS, -1, next_of)
    next_e = jnp.sum(jnp.where(block_e[:, None] == experts[None, :], next_of[None, :], 0), axis=1).astype(jnp.int32)
    n_used = (pad_ends[-1:] // rows).astype(jnp.int32)
    dest0, dest1 = dest[:n_tok], dest[n_tok:]
    return dest0, dest1, _invert(dest0, dest1, n_pad), block_e, next_e, n_used


def kernel(x, mem, w_in, p_moba, p_ret, p_mem, w_mem_kv, w_o, ln1_g, ln1_b, w_group, b_group,
           w_expert, b_expert, w_gate_up, w_down, ln2_g, ln2_b):
    bsz, seq, d = x.shape
    mem_len = mem.shape[1]
    n_tok = bsz * seq
    xf = x.reshape(n_tok, d)
    xb = xf.astype(BF16)
    mem_b = mem.reshape(bsz * mem_len, d).astype(BF16)
    for l in range(w_in.shape[0]):
        proj = _matmul(xb, w_in, l, BF16, 1024, 1024)
        kvm = _matmul(mem_b, w_mem_kv, l, BF16, 512, 1024)
        y_a = _moba(proj, bsz, seq)
        y_r = _retention(proj, bsz, seq)
        y_m = _mem_attention(proj, kvm, bsz, seq, mem_len)
        merged = _merge(y_a, y_r, y_m, p_moba, p_ret, p_mem, l, proj)
        x1, ids, wts = _wo_ln_router(merged, w_o, l, xf, ln1_g[l], ln1_b[l],
                                     w_group[l], b_group[l], w_expert[l], b_expert[l])
        dest0, dest1, row_tok, block_e, next_e, n_used = _slot_plan(ids, n_tok)
        y_rows = _experts(x1, row_tok, block_e, next_e, n_used, w_gate_up, w_down, l)
        xf, xb = _combine_ln(y_rows, dest0, dest1, wts[:EXPERT_TOPK].T, x1, ln2_g[l], ln2_b[l])
    return xf.reshape(bsz, seq, d)
```

```python
import functools

import jax
import jax.numpy as jnp
from jax import lax
from jax.experimental import pallas as pl
from jax.experimental.pallas import tpu as pltpu

F32 = jnp.float32
BF16 = jnp.bfloat16

D_MODEL = 2048
DEPTH = 2
MOBA_HEADS = 8
MOBA_HEAD_DIM = 128
MOBA_BLOCK = 256
MOBA_TOPK = 3
RET_HEADS = 4
RET_KEY_DIM = 256
RET_VALUE_DIM = 512
RET_CHUNK = 128
ROPE_BASE = 10000.0
MEM_HEADS = 4
MEM_HEAD_DIM = 256
N_BRANCHES = 3
MOBA_WIDTH = MOBA_HEADS * MOBA_HEAD_DIM
RET_QK_WIDTH = RET_HEADS * RET_KEY_DIM
RET_V_WIDTH = RET_HEADS * RET_VALUE_DIM
MEM_WIDTH = MEM_HEADS * MEM_HEAD_DIM
N_GROUPS = 4
EXPERTS_PER_GROUP = 8
N_EXPERTS = N_GROUPS * EXPERTS_PER_GROUP
EXPERT_TOPK = 2
EXPERT_FF = 512
DEEPNORM_ALPHA = (2 * DEPTH) ** 0.25
LN_EPS = 1e-5
GN_EPS = 1e-6
NEG_INF = -1e30
LOG2_E = 1.4426950408889634

OFF_AQ = 0
OFF_AK = OFF_AQ + MOBA_WIDTH
OFF_AV = OFF_AK + MOBA_WIDTH
OFF_RQ = OFF_AV + MOBA_WIDTH
OFF_RK = OFF_RQ + RET_QK_WIDTH
OFF_RV = OFF_RK + RET_QK_WIDTH
OFF_RG = OFF_RV + RET_V_WIDTH
OFF_MQ = OFF_RG + RET_V_WIDTH
OFF_GA = OFF_MQ + MEM_WIDTH
OFF_GR = OFF_GA + D_MODEL
OFF_GM = OFF_GR + D_MODEL

LANES = 128
SUBLANES = 8
VMEM_LIMIT = 56 * 1024 * 1024

EXPERT_ROWS = 256
ROUTER_ROWS = 48
ROW_TILE = 256

_NT = (((1,), (1,)), ((), ()))


def _params(*sem):
    return pltpu.CompilerParams(dimension_semantics=sem, vmem_limit_bytes=VMEM_LIMIT)


def _mm_kernel(a_ref, w_ref, o_ref, wb_ref):
    @pl.when(pl.program_id(1) == 0)
    def _():
        wb_ref[...] = w_ref[...].astype(BF16)

    o_ref[...] = jnp.dot(a_ref[...], wb_ref[...], preferred_element_type=F32).astype(o_ref.dtype)


def _matmul(a, w, layer, out_dtype, tm, tn):
    m, k = a.shape
    n = w.shape[2]
    tm, tn = min(tm, m), min(tn, n)
    return pl.pallas_call(
        _mm_kernel,
        grid=(n // tn, m // tm),
        in_specs=[pl.BlockSpec((tm, k), lambda j, i: (i, 0)),
                  pl.BlockSpec((None, k, tn), lambda j, i: (layer, 0, j))],
        out_specs=pl.BlockSpec((tm, tn), lambda j, i: (i, j)),
        out_shape=jax.ShapeDtypeStruct((m, n), out_dtype),
        scratch_shapes=[pltpu.VMEM((k, tn), BF16)],
        compiler_params=_params("parallel", "arbitrary"),
        name="matmul",
    )(a, w)


def _moba_kernel(q_ref, k_ref, v_ref, o_ref, vt_ref, *, n_blk, blk, top_n, scale):
    seq = n_blk * blk
    dh = q_ref.shape[1]
    shift = blk.bit_length() - 1

    row = lax.broadcasted_iota(jnp.int32, (2 * SUBLANES, seq), 0)
    col_blk = lax.shift_right_logical(lax.broadcasted_iota(jnp.int32, (2 * SUBLANES, seq), 1), shift)
    ind = jnp.where(col_blk == (row & (SUBLANES - 1)), 1.0, 0.0).astype(BF16)
    kmean = jnp.dot(ind, k_ref[...], preferred_element_type=F32) * (1.0 / blk)
    hi = kmean.astype(BF16).astype(F32)
    r2 = lax.broadcasted_iota(jnp.int32, (2 * SUBLANES, dh), 0)
    km = jnp.where(r2 < SUBLANES, hi, kmean - hi).astype(BF16)
    for j in range(n_blk):
        vt_ref[:, j * blk:(j + 1) * blk] = v_ref[j * blk:(j + 1) * blk, :].astype(F32).T.astype(BF16)

    g16 = lax.dot_general(km, q_ref[...], _NT, preferred_element_type=F32)
    g = g16[0:SUBLANES] + g16[SUBLANES:2 * SUBLANES]
    jrow = lax.broadcasted_iota(jnp.int32, (SUBLANES, seq), 0)
    q_blk = lax.shift_right_logical(lax.broadcasted_iota(jnp.int32, (SUBLANES, seq), 1), shift)
    past = jrow < q_blk
    g = jnp.where(past, g, NEG_INF)
    rank = jnp.zeros((SUBLANES, seq), F32)
    for jp in range(SUBLANES):
        gj = g[jp:jp + 1, :]
        rank = rank + jnp.where(gj > g, 1.0, jnp.where((gj == g) & (jp < jrow), 1.0, 0.0))
    sel = jnp.where(past & (rank < top_n), 1.0, 0.0)

    causal = (lax.broadcasted_iota(jnp.int32, (blk, blk), 0)
              <= lax.broadcasted_iota(jnp.int32, (blk, blk), 1))
    def scores(i):
        return lax.dot_general(k_ref[0:(i + 1) * blk, :], q_ref[i * blk:(i + 1) * blk, :], _NT,
                               preferred_element_type=F32)

    def emit(i, p_all, denom):
        acc = jnp.dot(vt_ref[:, 0:(i + 1) * blk], p_all, preferred_element_type=F32)
        o_ref[i * blk:(i + 1) * blk, :] = (acc * (1.0 / denom)).T.astype(o_ref.dtype)

    s_next = scores(0)
    pending = None
    for i in range(n_blk):
        lo, hi_q = i * blk, (i + 1) * blk
        s = s_next
        if i + 1 < n_blk:
            s_next = scores(i + 1)
        parts = [jnp.where(sel[j:j + 1, lo:hi_q] > 0.5, s[j * blk:(j + 1) * blk], NEG_INF) for j in range(i)]
        parts.append(jnp.where(causal, s[lo:hi_q], NEG_INF))
        m = functools.reduce(jnp.maximum, [jnp.max(p, axis=0, keepdims=True) for p in parts])
        probs = [jnp.exp2((p - m) * (scale * LOG2_E)) for p in parts]
        denom = functools.reduce(jnp.add, [jnp.sum(p, axis=0, keepdims=True) for p in probs])
        p_all = jnp.concatenate([p.astype(BF16) for p in probs], axis=0)
        if pending is not None:
            emit(*pending)
        pending = (i, p_all, denom)
    emit(*pending)


def _moba(proj, bsz, seq):
    n_blk = seq // MOBA_BLOCK
    assert seq % MOBA_BLOCK == 0 and n_blk <= SUBLANES
    dh = MOBA_HEAD_DIM
    kern = functools.partial(_moba_kernel, n_blk=n_blk, blk=MOBA_BLOCK, top_n=min(MOBA_TOPK, n_blk),
                             scale=dh ** -0.5)
    return pl.pallas_call(
        kern,
        grid=(bsz, MOBA_HEADS),
        in_specs=[pl.BlockSpec((seq, dh), lambda b, h: (b, OFF_AQ // dh + h)),
                  pl.BlockSpec((seq, dh), lambda b, h: (b, OFF_AK // dh + h)),
                  pl.BlockSpec((seq, dh), lambda b, h: (b, OFF_AV // dh + h))],
        out_specs=pl.BlockSpec((seq, dh), lambda b, h: (b, h)),
        out_shape=jax.ShapeDtypeStruct((bsz * seq, MOBA_WIDTH), BF16),
        scratch_shapes=[pltpu.VMEM((dh, seq), BF16)],
        compiler_params=_params("parallel", "parallel"),
        name="moba",
    )(proj, proj, proj)


def _ret_kernel(cd_ref, q_ref, k_ref, v_ref, g_ref, cos_ref, sin_ref, dec_ref, zeta_ref, xi_ref,
                o_ref, st_ref, *, dk, c):
    cd = cd_ref[pl.program_id(1)]
    n_c = q_ref.shape[0] // c
    half = dk // 2

    def widen(t):
        return jnp.concatenate([t] * (dk // LANES), axis=1)

    dec = dec_ref[...]
    xi = widen(xi_ref[...])
    zeta = widen(zeta_ref[...])

    def within_chunk(n):
        r = slice(n * c, (n + 1) * c)
        cos = cos_ref[r, :]
        sin = sin_ref[r, :]

        def rot(x):
            x1 = x[:, :half]
            x2 = x[:, half:]
            return jnp.concatenate([x1 * cos - x2 * sin, x1 * sin + x2 * cos], axis=1)

        q = rot(q_ref[r, :].astype(F32))
        k = rot(k_ref[r, :].astype(F32)) * (dk ** -0.5)
        v = v_ref[r, :]
        sc = lax.dot_general(q.astype(BF16), k.astype(BF16), _NT, preferred_element_type=F32) * dec
        inner = jnp.dot(sc.astype(BF16), v, preferred_element_type=F32)
        kv = jnp.dot((k * zeta).T.astype(BF16), v, preferred_element_type=F32) if n < n_c - 1 else None
        return inner, kv, (q * xi).astype(BF16)

    ahead = within_chunk(0)
    for n in range(n_c):
        r = slice(n * c, (n + 1) * c)
        o, kv, qx = ahead
        if n + 1 < n_c:
            ahead = within_chunk(n + 1)
        if n > 0:
            o = o + jnp.dot(qx, st_ref[...].astype(BF16), preferred_element_type=F32)
        if kv is not None:
            st_ref[...] = kv if n == 0 else cd * st_ref[...] + kv

        mu = jnp.mean(o, axis=1, keepdims=True)
        d = o - mu
        var = jnp.mean(d * d, axis=1, keepdims=True)
        gg = g_ref[r, :].astype(F32)
        half_g = 0.5 * gg
        o_ref[r, :] = ((half_g + half_g * jnp.tanh(half_g)) * (d * lax.rsqrt(var + GN_EPS))).astype(o_ref.dtype)


def _retention(proj, bsz, seq):
    c, dk, dv, nh = RET_CHUNK, RET_KEY_DIM, RET_VALUE_DIM, RET_HEADS
    n_c = seq // c
    half = dk // 2
    pos = jnp.arange(seq, dtype=F32)
    inv = ROPE_BASE ** (-jnp.linspace(0.0, 1.0, half, dtype=F32))
    ang = pos[:, None] * inv[None, :]
    cos, sin = jnp.cos(ang), jnp.sin(ang)
    log_g = jnp.log1p(-jnp.power(2.0, -5.0 - jnp.arange(nh, dtype=F32)))
    idx = jnp.arange(c, dtype=F32)
    diff = idx[:, None] - idx[None, :]
    decay = jnp.where(diff >= 0, jnp.exp(log_g[:, None, None] * jnp.maximum(diff, 0.0)), 0.0)
    zeta = jnp.exp(log_g[:, None] * (c - 1 - idx)[None, :])
    xi = jnp.exp(log_g[:, None] * (idx + 1.0)[None, :])
    zeta_b = jnp.broadcast_to(zeta[:, :, None], (nh, c, LANES))
    xi_b = jnp.broadcast_to(xi[:, :, None], (nh, c, LANES))
    chunk_decay = jnp.exp(log_g * c)

    return pl.pallas_call(
        functools.partial(_ret_kernel, dk=dk, c=c),
        grid=(bsz, nh),
        in_specs=[pl.BlockSpec(memory_space=pltpu.SMEM),
                  pl.BlockSpec((seq, dk), lambda b, h: (b, OFF_RQ // dk + h)),
                  pl.BlockSpec((seq, dk), lambda b, h: (b, OFF_RK // dk + h)),
                  pl.BlockSpec((seq, dv), lambda b, h: (b, OFF_RV // dv + h)),
                  pl.BlockSpec((seq, dv), lambda b, h: (b, OFF_RG // dv + h)),
                  pl.BlockSpec((seq, half), lambda b, h: (0, 0)),
                  pl.BlockSpec((seq, half), lambda b, h: (0, 0)),
                  pl.BlockSpec((None, c, c), lambda b, h: (h, 0, 0)),
                  pl.BlockSpec((None, c, LANES), lambda b, h: (h, 0, 0)),
                  pl.BlockSpec((None, c, LANES), lambda b, h: (h, 0, 0))],
        out_specs=pl.BlockSpec((seq, dv), lambda b, h: (b, h)),
        out_shape=jax.ShapeDtypeStruct((bsz * seq, RET_V_WIDTH), BF16),
        scratch_shapes=[pltpu.VMEM((dk, dv), F32)],
        compiler_params=_params("parallel", "parallel"),
        name="retention",
    )(chunk_decay, proj, proj, proj, proj, cos, sin, decay, zeta_b, xi_b)


def _mem_kernel(q_ref, k_ref, v_ref, o_ref, *, scale, sub):
    n_sub = q_ref.shape[0] // sub

    def scores(t):
        return lax.dot_general(q_ref[t * sub:(t + 1) * sub, :], k_ref[...], _NT, preferred_element_type=F32)

    s_next = scores(0)
    for t in range(n_sub):
        s = s_next * scale
        if t + 1 < n_sub:
            s_next = scores(t + 1)
        m = jnp.max(s, axis=1, keepdims=True)
        p = jnp.exp(s - m)
        p = p * (1.0 / jnp.sum(p, axis=1, keepdims=True))
        o_ref[t * sub:(t + 1) * sub, :] = jnp.dot(p.astype(BF16), v_ref[...],
                                                  preferred_element_type=F32).astype(o_ref.dtype)


def _mem_attention(proj, kvm, bsz, seq, mem_len):
    dh, nh = MEM_HEAD_DIM, MEM_HEADS
    tq = min(2048, seq)
    n_q = seq // tq
    return pl.pallas_call(
        functools.partial(_mem_kernel, scale=dh ** -0.5, sub=min(512, tq)),
        grid=(bsz, nh, n_q),
        in_specs=[pl.BlockSpec((tq, dh), lambda b, h, i: (b * n_q + i, OFF_MQ // dh + h)),
                  pl.BlockSpec((mem_len, dh), lambda b, h, i: (b, h)),
                  pl.BlockSpec((mem_len, dh), lambda b, h, i: (b, nh + h))],
        out_specs=pl.BlockSpec((tq, dh), lambda b, h, i: (b * n_q + i, h)),
        out_shape=jax.ShapeDtypeStruct((bsz * seq, MEM_WIDTH), BF16),
        compiler_params=_params("parallel", "parallel", "parallel"),
        name="mem_attention",
    )(proj, kvm, kvm)


def _merge_kernel(ya_ref, yr_ref, ym_ref, pa_ref, pr_ref, pm_ref, ga_ref, gr_ref, gm_ref, o_ref,
                  pab_ref, prb_ref, pmb_ref):
    @pl.when(pl.program_id(1) == 0)
    def _():
        pab_ref[...] = pa_ref[...].astype(BF16)
        prb_ref[...] = pr_ref[...].astype(BF16)
        pmb_ref[...] = pm_ref[...].astype(BF16)

    def branch(y_ref, p_ref, gate_ref):
        t = jnp.dot(y_ref[...], p_ref[...], preferred_element_type=F32)
        return jax.nn.sigmoid(gate_ref[...].astype(F32)) * t

    o_ref[...] = (branch(ya_ref, pab_ref, ga_ref) + branch(yr_ref, prb_ref, gr_ref)
                  + branch(ym_ref, pmb_ref, gm_ref)).astype(o_ref.dtype)


def _merge(y_a, y_r, y_m, p_moba, p_ret, p_mem, layer, proj):
    m = y_a.shape[0]
    tm, tn = min(512, m), 1024
    d = D_MODEL

    def rows(width):
        return pl.BlockSpec((tm, width), lambda j, i: (i, 0))

    def wcols(kdim):
        return pl.BlockSpec((None, kdim, tn), lambda j, i: (layer, 0, j), pipeline_mode=pl.Buffered(1))

    def gate(off):
        return pl.BlockSpec((tm, tn), lambda j, i: (i, off // tn + j))

    return pl.pallas_call(
        _merge_kernel,
        grid=(d // tn, m // tm),
        in_specs=[rows(MOBA_WIDTH), rows(RET_V_WIDTH), rows(MEM_WIDTH),
                  wcols(MOBA_WIDTH), wcols(RET_V_WIDTH), wcols(MEM_WIDTH),
                  gate(OFF_GA), gate(OFF_GR), gate(OFF_GM)],
        out_specs=pl.BlockSpec((tm, tn), lambda j, i: (i, j)),
        out_shape=jax.ShapeDtypeStruct((m, d), BF16),
        scratch_shapes=[pltpu.VMEM((MOBA_WIDTH, tn), BF16), pltpu.VMEM((RET_V_WIDTH, tn), BF16),
                        pltpu.VMEM((MEM_WIDTH, tn), BF16)],
        compiler_params=_params("parallel", "arbitrary"),
        name="merge",
    )(y_a, y_r, y_m, p_moba, p_ret, p_mem, proj, proj, proj)


def _layer_norm_rows(z, g, b):
    mu = jnp.mean(z, axis=1, keepdims=True)
    d = z - mu
    var = jnp.mean(d * d, axis=1, keepdims=True)
    return d * lax.rsqrt(var + LN_EPS) * g + b


def _wo_ln_router_kernel(mg_ref, wo_hbm, x_ref, g_ref, b_ref, wr2_ref, wr1_ref, br_ref,
                         x1_ref, ids_ref, wts_ref, wob_ref, stage_ref, sem, *, layer, sub):
    @pl.when(pl.program_id(0) == 0)
    def _():
        chunk = stage_ref.shape[0]
        for c in range(wob_ref.shape[0] // chunk):
            cp = pltpu.make_async_copy(wo_hbm.at[layer, pl.ds(c * chunk, chunk)], stage_ref, sem)
            cp.start()
            cp.wait()
            wob_ref[c * chunk:(c + 1) * chunk, :] = stage_ref[...].astype(BF16)

    n_sub = mg_ref.shape[0] // sub

    def mix_of(h):
        return jnp.dot(mg_ref[h * sub:(h + 1) * sub, :], wob_ref[...], preferred_element_type=F32)

    mix_next = mix_of(0)
    for h in range(n_sub):
        rows = slice(h * sub, (h + 1) * sub)
        mix = mix_next
        if h + 1 < n_sub:
            mix_next = mix_of(h + 1)
        _ln_router_rows(mix, x_ref[rows, :], g_ref, b_ref, wr2_ref, wr1_ref, br_ref,
                        x1_ref.at[rows, :], ids_ref.at[:, rows], wts_ref.at[:, rows])


def _ln_router_rows(mix, x, g_ref, b_ref, wr2_ref, wr1_ref, br_ref, x1_ref, ids_ref, wts_ref):
    x1 = _layer_norm_rows(DEEPNORM_ALPHA * x + mix, g_ref[...], b_ref[...])
    x1_ref[...] = x1

    r = ROUTER_ROWS
    x_hi = x1.astype(BF16)
    x_lo = (x1 - x_hi.astype(F32)).astype(BF16)
    l2 = lax.dot_general(wr2_ref[...], x_hi, _NT, preferred_element_type=F32)
    l1 = lax.dot_general(wr1_ref[...], x_lo, _NT, preferred_element_type=F32)
    logit = l2[0:r] + l2[r:2 * r] + l1 + br_ref[...]

    tok = logit.shape[1]
    row = lax.broadcasted_iota(jnp.int32, (SUBLANES, tok), 0).astype(F32)
    gl = jnp.where(row < N_GROUPS, logit[0:SUBLANES], NEG_INF)
    g_max = jnp.max(gl, axis=0, keepdims=True)
    g_sel = jnp.min(jnp.where(gl == g_max, row, float(SUBLANES)), axis=0, keepdims=True)
    p_g = 1.0 / jnp.sum(jnp.exp(gl - g_max), axis=0, keepdims=True)
    e_in = jnp.zeros((SUBLANES, tok), F32)
    for grp in range(N_GROUPS):
        lo = SUBLANES * (1 + grp)
        e_in = jnp.where(g_sel == float(grp), logit[lo:lo + EXPERTS_PER_GROUP], e_in)
    v1 = jnp.max(e_in, axis=0, keepdims=True)
    i1 = jnp.min(jnp.where(e_in == v1, row, float(SUBLANES)), axis=0, keepdims=True)
    e_rest = jnp.where(row == i1, -jnp.inf, e_in)
    v2 = jnp.max(e_rest, axis=0, keepdims=True)
    i2 = jnp.min(jnp.where(e_rest == v2, row, float(SUBLANES)), axis=0, keepdims=True)
    t = jnp.exp(v2 - v1)
    w1 = p_g / (1.0 + t)
    w2 = w1 * t
    e1 = g_sel * float(EXPERTS_PER_GROUP) + i1
    e2 = g_sel * float(EXPERTS_PER_GROUP) + i2
    ids_ref[...] = jnp.where(row == 0.0, e1, jnp.where(row == 1.0, e2, 0.0)).astype(jnp.int32)
    wts_ref[...] = jnp.where(row == 0.0, w1, jnp.where(row == 1.0, w2, 0.0))


def _wo_ln_router(merged, w_o, layer, x, ln_g, ln_b, w_group, b_group, w_expert, b_expert):
    m, d = x.shape
    sub = min(ROW_TILE, m)
    tm = min(2 * ROW_TILE, m)
    r = ROUTER_ROWS
    wr = jnp.zeros((r, d), F32).at[0:N_GROUPS].set(w_group.T).at[SUBLANES:SUBLANES + N_EXPERTS].set(w_expert.T)
    br = jnp.zeros((r, 1), F32).at[0:N_GROUPS, 0].set(b_group).at[SUBLANES:SUBLANES + N_EXPERTS, 0].set(b_expert)
    wr_hi = wr.astype(BF16)
    wr_lo = (wr - wr_hi.astype(F32)).astype(BF16)
    wr2 = jnp.concatenate([wr_hi, wr_lo], axis=0)

    def whole(shape):
        return pl.BlockSpec(shape, lambda i: (0,) * len(shape))

    return pl.pallas_call(
        functools.partial(_wo_ln_router_kernel, layer=layer, sub=sub),
        grid=(m // tm,),
        in_specs=[pl.BlockSpec((tm, d), lambda i: (i, 0)),
                  pl.BlockSpec(memory_space=pl.ANY),
                  pl.BlockSpec((tm, d), lambda i: (i, 0)),
                  whole((1, d)), whole((1, d)),
                  whole((2 * r, d)), whole((r, d)), whole((r, 1))],
        out_specs=[pl.BlockSpec((tm, d), lambda i: (i, 0)),
                   pl.BlockSpec((SUBLANES, tm), lambda i: (0, i)),
                   pl.BlockSpec((SUBLANES, tm), lambda i: (0, i))],
        out_shape=[jax.ShapeDtypeStruct((m, d), F32),
                   jax.ShapeDtypeStruct((SUBLANES, m), jnp.int32),
                   jax.ShapeDtypeStruct((SUBLANES, m), F32)],
        scratch_shapes=[pltpu.VMEM((d, d), BF16), pltpu.VMEM((d // 4, d), F32), pltpu.SemaphoreType.DMA(())],
        compiler_params=_params("arbitrary"),
        name="wo_ln_router",
    )(merged, w_o, x, ln_g.reshape(1, d), ln_b.reshape(1, d), wr2, wr_hi, br)


def _row_copy(src, src_row, dst, dst_row, sem):
    return pltpu.make_async_copy(src.at[pl.ds(src_row, 1)], dst.at[pl.ds(dst_row, 1)], sem)


def _rank_kernel(e_ref, pos_ref, cnt_ref, tri_ref, carry_ref):
    n = e_ref.shape[1]

    @pl.when(pl.program_id(0) == 0)
    def _():
        r = lax.broadcasted_iota(jnp.int32, (n, n), 0)
        c = lax.broadcasted_iota(jnp.int32, (n, n), 1)
        tri_ref[...] = jnp.where(r <= c, 1.0, 0.0).astype(BF16)
        carry_ref[...] = jnp.zeros_like(carry_ref)

    row = lax.broadcasted_iota(jnp.int32, (N_EXPERTS, n), 0)
    onehot = jnp.where(row == e_ref[...], 1.0, 0.0)
    prefix = jnp.dot(onehot.astype(BF16), tri_ref[...], preferred_element_type=F32)
    carry = carry_ref[...]
    pos = jnp.sum(onehot * (prefix + carry[:, 0:1]), axis=0, keepdims=True) - 1.0
    pos_ref[...] = pos.astype(jnp.int32)
    carry = carry + jnp.sum(onehot, axis=1, keepdims=True)
    carry_ref[...] = carry
    cnt_ref[...] = carry


def _rank(flat_e):
    n_asg = flat_e.shape[1]
    tile = min(2048, n_asg)
    return pl.pallas_call(
        _rank_kernel,
        grid=(n_asg // tile,),
        in_specs=[pl.BlockSpec((1, tile), lambda i: (0, i))],
        out_specs=[pl.BlockSpec((1, tile), lambda i: (0, i)),
                   pl.BlockSpec((N_EXPERTS, LANES), lambda i: (0, 0))],
        out_shape=[jax.ShapeDtypeStruct((1, n_asg), jnp.int32),
                   jax.ShapeDtypeStruct((N_EXPERTS, LANES), F32)],
        scratch_shapes=[pltpu.VMEM((tile, tile), BF16), pltpu.VMEM((N_EXPERTS, LANES), F32)],
        compiler_params=_params("arbitrary"),
        name="rank",
    )(flat_e)


def _invert_kernel(d0_ref, d1_ref, zeros_hbm, o_ref, sem):
    clear = pltpu.make_async_copy(zeros_hbm, o_ref, sem)
    clear.start()
    clear.wait()

    def put(t, c):
        o_ref[d0_ref[t]] = t
        o_ref[d1_ref[t]] = t
        return c

    lax.fori_loop(0, d0_ref.shape[0], put, 0, unroll=8)


def _invert(dest0, dest1, n_pad):
    smem = pl.BlockSpec(memory_space=pltpu.SMEM)
    return pl.pallas_call(
        _invert_kernel,
        in_specs=[smem, smem, pl.BlockSpec(memory_space=pl.ANY)],
        out_specs=smem,
        out_shape=jax.ShapeDtypeStruct((n_pad,), jnp.int32),
        scratch_shapes=[pltpu.SemaphoreType.DMA(())],
        name="invert",
    )(dest0, dest1, jnp.zeros((n_pad,), jnp.int32))


GATHER_SLOTS = 3


def _expert_kernel(be_ref, nx_ref, nu_ref, rt_ref, x_hbm, wgu_hbm, wd_hbm, o_ref,
                   xbuf, wgu_s, wd_s, wgu_b, wd_b, gsem, wsem, *, rows, n_blocks, layer):
    i = pl.program_id(0)
    nu = nu_ref[0]
    ff = wd_b.shape[0]
    ns = GATHER_SLOTS

    def issue(blk, slot):
        base = blk * rows
        for r in range(rows):
            _row_copy(x_hbm, rt_ref[base + r], xbuf.at[slot], r, gsem.at[slot]).start()

    def drain(slot):
        for _ in range(rows):
            _row_copy(x_hbm, 0, xbuf.at[slot], 0, gsem.at[slot]).wait()

    def weight_copies(e):
        return (pltpu.make_async_copy(wgu_hbm.at[layer, e], wgu_s, wsem.at[0]),
                pltpu.make_async_copy(wd_hbm.at[layer, e], wd_s, wsem.at[1]))

    @pl.when(i == 0)
    def _():
        for cp in weight_copies(be_ref[0]):
            cp.start()
        issue(0, 0)
        issue(min(1, n_blocks - 1), 1)

    @pl.when((i < nu) & ((i == 0) | (be_ref[i] != be_ref[jnp.maximum(i - 1, 0)])))
    def _():
        for cp in weight_copies(be_ref[i]):
            cp.wait()
        wgu_b[...] = wgu_s[...].astype(BF16)
        wd_b[...] = wd_s[...].astype(BF16)

        @pl.when(nx_ref[i] >= 0)
        def _():
            for cp in weight_copies(nx_ref[i]):
                cp.start()

    def compute(slot):
        drain(slot)
        xb = xbuf[slot].astype(BF16)
        issue(jnp.minimum(i + 2, n_blocks - 1), (slot + 2) % ns)
        gu = jnp.dot(xb, wgu_b[...], preferred_element_type=F32)
        gate = gu[:, :ff]
        hmid = gate * jax.nn.sigmoid(gate) * gu[:, ff:]
        o_ref[...] = jnp.dot(hmid.astype(BF16), wd_b[...], preferred_element_type=F32)

    phase = lax.rem(i, ns)
    for slot in range(ns):
        @pl.when((i < nu) & (phase == slot))
        def _(slot=slot):
            compute(slot)

    for slot in range(ns):
        @pl.when((i == nu) & (phase == slot))
        def _(slot=slot):
            drain(slot)
            drain((slot + 1) % ns)

    @pl.when(i >= nu)
    def _():
        o_ref[...] = jnp.zeros_like(o_ref)


def _experts(x1, row_tok, block_e, next_e, n_used, w_gate_up, w_down, layer):
    d = x1.shape[1]
    n_pad = row_tok.shape[0]
    ff = w_down.shape[2]
    rows = EXPERT_ROWS
    n_blocks = n_pad // rows
    assert n_blocks >= 2
    return pl.pallas_call(
        functools.partial(_expert_kernel, rows=rows, n_blocks=n_blocks, layer=layer),
        grid_spec=pltpu.PrefetchScalarGridSpec(
            num_scalar_prefetch=4,
            grid=(n_blocks,),
            in_specs=[pl.BlockSpec(memory_space=pl.ANY),
                      pl.BlockSpec(memory_space=pl.ANY),
                      pl.BlockSpec(memory_space=pl.ANY)],
            out_specs=pl.BlockSpec((rows, d), lambda i, be, nx, nu, rt: (i, 0)),
            scratch_shapes=[pltpu.VMEM((GATHER_SLOTS, rows, d), F32),
                            pltpu.VMEM((d, 2 * ff), F32),
                            pltpu.VMEM((ff, d), F32),
                            pltpu.VMEM((d, 2 * ff), BF16),
                            pltpu.VMEM((ff, d), BF16),
                            pltpu.SemaphoreType.DMA((GATHER_SLOTS,)),
                            pltpu.SemaphoreType.DMA((2,))]),
        out_shape=jax.ShapeDtypeStruct((n_pad, d), F32),
        compiler_params=_params("arbitrary"),
        name="experts",
    )(block_e, next_e, n_used, row_tok, x1, w_gate_up, w_down)


def _combine_ln_kernel(d0_ref, d1_ref, y_hbm, w_ref, x_ref, g_ref, b_ref, o_ref, ob_ref, buf, sem,
                       *, tb, n_steps):
    i = pl.program_id(0)

    def issue(step, slot):
        base = step * tb
        for t in range(tb):
            _row_copy(y_hbm, d0_ref[base + t], buf.at[slot, 0], t, sem.at[slot]).start()
            _row_copy(y_hbm, d1_ref[base + t], buf.at[slot, 1], t, sem.at[slot]).start()

    def drain(slot):
        for _ in range(EXPERT_TOPK * tb):
            _row_copy(y_hbm, 0, buf.at[slot, 0], 0, sem.at[slot]).wait()

    @pl.when(i == 0)
    def _():
        issue(0, 0)

    def compute(slot):
        drain(slot)
        w = w_ref[...]
        ffn = w[:, 0:1] * buf[slot, 0] + w[:, 1:2] * buf[slot, 1]
        out = _layer_norm_rows(DEEPNORM_ALPHA * x_ref[...] + ffn, g_ref[...], b_ref[...])
        o_ref[...] = out
        ob_ref[...] = out.astype(BF16)

    for slot in range(2):
        @pl.when((i % 2 == slot) & (i + 1 < n_steps))
        def _(slot=slot):
            issue(i + 1, 1 - slot)

        @pl.when(i % 2 == slot)
        def _(slot=slot):
            compute(slot)


def _combine_ln(y_rows, dest0, dest1, w_tok, x1, ln_g, ln_b):
    m, d = x1.shape
    tb = min(256, m)
    n_steps = m // tb
    return pl.pallas_call(
        functools.partial(_combine_ln_kernel, tb=tb, n_steps=n_steps),
        grid_spec=pltpu.PrefetchScalarGridSpec(
            num_scalar_prefetch=2,
            grid=(n_steps,),
            in_specs=[pl.BlockSpec(memory_space=pl.ANY),
                      pl.BlockSpec((tb, EXPERT_TOPK), lambda i, a, b: (i, 0)),
                      pl.BlockSpec((tb, d), lambda i, a, b: (i, 0)),
                      pl.BlockSpec((1, d), lambda i, a, b: (0, 0)),
                      pl.BlockSpec((1, d), lambda i, a, b: (0, 0))],
            out_specs=[pl.BlockSpec((tb, d), lambda i, a, b: (i, 0)),
                       pl.BlockSpec((tb, d), lambda i, a, b: (i, 0))],
            scratch_shapes=[pltpu.VMEM((2, EXPERT_TOPK, tb, d), F32), pltpu.SemaphoreType.DMA((2,))]),
        out_shape=[jax.ShapeDtypeStruct((m, d), F32), jax.ShapeDtypeStruct((m, d), BF16)],
        compiler_params=_params("arbitrary"),
        name="combine_ln",
    )(dest0, dest1, y_rows, w_tok, x1, ln_g.reshape(1, d), ln_b.reshape(1, d))


def _slot_plan(ids, n_tok):
    rows = EXPERT_ROWS
    n_asg = n_tok * EXPERT_TOPK
    n_pad = n_asg + N_EXPERTS * rows
    flat_e = ids[:EXPERT_TOPK].reshape(1, n_asg)
    pos, cnt = _rank(flat_e)
    counts = cnt[:, 0].astype(jnp.int32)
    padded = ((counts + rows - 1) // rows) * rows
    pad_ends = jnp.cumsum(padded)
    pad_starts = pad_ends - padded
    experts = jnp.arange(N_EXPERTS, dtype=jnp.int32)
    start_of = jnp.sum(jnp.where(flat_e[0][:, None] == experts[None, :], pad_starts[None, :], 0), axis=1)
    dest = (start_of + pos[0]).astype(jnp.int32)
    n_blocks = n_pad // rows
    block_start = jnp.arange(n_blocks, dtype=jnp.int32) * rows
    block_e = jnp.minimum(jnp.sum((pad_ends[None, :] <= block_start[:, None]).astype(jnp.int32), axis=1),
                          N_EXPERTS - 1)
    later = (experts[None, :] > experts[:, None]) & (counts[None, :] > 0)
    next_of = jnp.min(jnp.where(later, experts[None, :], N_EXPERTS), axis=1)
    next_of = jnp.where(next_of == N_EXPERTS, -1, next_of)
    next_e = jnp.sum(jnp.where(block_e[:, None] == experts[None, :], next_of[None, :], 0), axis=1).astype(jnp.int32)
    n_used = (pad_ends[-1:] // rows).astype(jnp.int32)
    dest0, dest1 = dest[:n_tok], dest[n_tok:]
    return dest0, dest1, _invert(dest0, dest1, n_pad), block_e, next_e, n_used


def kernel(x, mem, w_in, p_moba, p_ret, p_mem, w_mem_kv, w_o, ln1_g, ln1_b, w_group, b_group,
           w_expert, b_expert, w_gate_up, w_down, ln2_g, ln2_b):
    bsz, seq, d = x.shape
    mem_len = mem.shape[1]
    n_tok = bsz * seq
    xf = x.reshape(n_tok, d)
    xb = xf.astype(BF16)
    mem_b = mem.reshape(bsz * mem_len, d).astype(BF16)
    for l in range(w_in.shape[0]):
        proj = _matmul(xb, w_in, l, BF16, 1024, 1024)
        kvm = _matmul(mem_b, w_mem_kv, l, BF16, 512, 1024)
        y_a = _moba(proj, bsz, seq)
        y_r = _retention(proj, bsz, seq)
        y_m = _mem_attention(proj, kvm, bsz, seq, mem_len)
        merged = _merge(y_a, y_r, y_m, p_moba, p_ret, p_mem, l, proj)
        x1, ids, wts = _wo_ln_router(merged, w_o, l, xf, ln1_g[l], ln1_b[l],
                                     w_group[l], b_group[l], w_expert[l], b_expert[l])
        dest0, dest1, row_tok, block_e, next_e, n_used = _slot_plan(ids, n_tok)
        y_rows = _experts(x1, row_tok, block_e, next_e, n_used, w_gate_up, w_down, l)
        xf, xb = _combine_ln(y_rows, dest0, dest1, wts[:EXPERT_TOPK].T, x1, ln2_g[l], ln2_b[l])
    return xf.reshape(bsz, seq, d)
```

```python
import functools

import jax
import jax.numpy as jnp
from jax import lax
from jax.experimental import pallas as pl
from jax.experimental.pallas import tpu as pltpu

F32 = jnp.float32
BF16 = jnp.bfloat16

D_MODEL = 2048
DEPTH = 2
MOBA_HEADS = 8
MOBA_HEAD_DIM = 128
MOBA_BLOCK = 256
MOBA_TOPK = 3
RET_HEADS = 4
RET_KEY_DIM = 256
RET_VALUE_DIM = 512
RET_CHUNK = 128
ROPE_BASE = 10000.0
MEM_HEADS = 4
MEM_HEAD_DIM = 256
N_BRANCHES = 3
MOBA_WIDTH = MOBA_HEADS * MOBA_HEAD_DIM
RET_QK_WIDTH = RET_HEADS * RET_KEY_DIM
RET_V_WIDTH = RET_HEADS * RET_VALUE_DIM
MEM_WIDTH = MEM_HEADS * MEM_HEAD_DIM
N_GROUPS = 4
EXPERTS_PER_GROUP = 8
N_EXPERTS = N_GROUPS * EXPERTS_PER_GROUP
EXPERT_TOPK = 2
EXPERT_FF = 512
DEEPNORM_ALPHA = (2 * DEPTH) ** 0.25
LN_EPS = 1e-5
GN_EPS = 1e-6
NEG_INF = -1e30
LOG2_E = 1.4426950408889634

OFF_AQ = 0
OFF_AK = OFF_AQ + MOBA_WIDTH
OFF_AV = OFF_AK + MOBA_WIDTH
OFF_RQ = OFF_AV + MOBA_WIDTH
OFF_RK = OFF_RQ + RET_QK_WIDTH
OFF_RV = OFF_RK + RET_QK_WIDTH
OFF_RG = OFF_RV + RET_V_WIDTH
OFF_MQ = OFF_RG + RET_V_WIDTH
OFF_GA = OFF_MQ + MEM_WIDTH
OFF_GR = OFF_GA + D_MODEL
OFF_GM = OFF_GR + D_MODEL

LANES = 128
SUBLANES = 8
VMEM_LIMIT = 56 * 1024 * 1024

EXPERT_ROWS = 256
ROUTER_ROWS = 48
ROW_TILE = 256

_NT = (((1,), (1,)), ((), ()))


def _params(*sem):
    return pltpu.CompilerParams(dimension_semantics=sem, vmem_limit_bytes=VMEM_LIMIT)


def _mm_kernel(a_ref, w_ref, o_ref, wb_ref):
    @pl.when(pl.program_id(1) == 0)
    def _():
        wb_ref[...] = w_ref[...].astype(BF16)

    a = a_ref[...]
    if a.dtype != BF16:
        a = a.astype(BF16)
    o_ref[...] = jnp.dot(a, wb_ref[...], preferred_element_type=F32).astype(o_ref.dtype)


def _matmul(a, w, layer, out_dtype, tm, tn):
    m, k = a.shape
    n = w.shape[2]
    tm, tn = min(tm, m), min(tn, n)
    return pl.pallas_call(
        _mm_kernel,
        grid=(n // tn, m // tm),
        in_specs=[pl.BlockSpec((tm, k), lambda j, i: (i, 0)),
                  pl.BlockSpec((None, k, tn), lambda j, i: (layer, 0, j))],
        out_specs=pl.BlockSpec((tm, tn), lambda j, i: (i, j)),
        out_shape=jax.ShapeDtypeStruct((m, n), out_dtype),
        scratch_shapes=[pltpu.VMEM((k, tn), BF16)],
        compiler_params=_params("parallel", "arbitrary"),
        name="matmul",
    )(a, w)


def _moba_kernel(q_ref, k_ref, v_ref, o_ref, vt_ref, *, n_blk, blk, top_n, scale):
    seq = n_blk * blk
    dh = q_ref.shape[1]
    shift = blk.bit_length() - 1

    row = lax.broadcasted_iota(jnp.int32, (2 * SUBLANES, seq), 0)
    col_blk = lax.shift_right_logical(lax.broadcasted_iota(jnp.int32, (2 * SUBLANES, seq), 1), shift)
    ind = jnp.where(col_blk == (row & (SUBLANES - 1)), 1.0, 0.0).astype(BF16)
    kmean = jnp.dot(ind, k_ref[...], preferred_element_type=F32) * (1.0 / blk)
    hi = kmean.astype(BF16).astype(F32)
    r2 = lax.broadcasted_iota(jnp.int32, (2 * SUBLANES, dh), 0)
    km = jnp.where(r2 < SUBLANES, hi, kmean - hi).astype(BF16)
    for j in range(n_blk):
        vt_ref[:, j * blk:(j + 1) * blk] = v_ref[j * blk:(j + 1) * blk, :].astype(F32).T.astype(BF16)

    g16 = lax.dot_general(km, q_ref[...], _NT, preferred_element_type=F32)
    g = g16[0:SUBLANES] + g16[SUBLANES:2 * SUBLANES]
    jrow = lax.broadcasted_iota(jnp.int32, (SUBLANES, seq), 0)
    q_blk = lax.shift_right_logical(lax.broadcasted_iota(jnp.int32, (SUBLANES, seq), 1), shift)
    past = jrow < q_blk
    g = jnp.where(past, g, NEG_INF)
    rank = jnp.zeros((SUBLANES, seq), F32)
    for jp in range(SUBLANES):
        gj = g[jp:jp + 1, :]
        rank = rank + jnp.where(gj > g, 1.0, jnp.where((gj == g) & (jp < jrow), 1.0, 0.0))
    sel = jnp.where(past & (rank < top_n), 1.0, 0.0)

    causal = (lax.broadcasted_iota(jnp.int32, (blk, blk), 0)
              <= lax.broadcasted_iota(jnp.int32, (blk, blk), 1))
    def scores(i):
        return lax.dot_general(k_ref[0:(i + 1) * blk, :], q_ref[i * blk:(i + 1) * blk, :], _NT,
                               preferred_element_type=F32)

    def emit(i, p_all, denom):
        acc = jnp.dot(vt_ref[:, 0:(i + 1) * blk], p_all, preferred_element_type=F32)
        o_ref[i * blk:(i + 1) * blk, :] = (acc * (1.0 / denom)).T.astype(o_ref.dtype)

    s_next = scores(0)
    pending = None
    for i in range(n_blk):
        lo, hi_q = i * blk, (i + 1) * blk
        s = s_next
        if i + 1 < n_blk:
            s_next = scores(i + 1)
        parts = [jnp.where(sel[j:j + 1, lo:hi_q] > 0.5, s[j * blk:(j + 1) * blk], NEG_INF) for j in range(i)]
        parts.append(jnp.where(causal, s[lo:hi_q], NEG_INF))
        m = functools.reduce(jnp.maximum, [jnp.max(p, axis=0, keepdims=True) for p in parts])
        probs = [jnp.exp2((p - m) * (scale * LOG2_E)) for p in parts]
        denom = functools.reduce(jnp.add, [jnp.sum(p, axis=0, keepdims=True) for p in probs])
        p_all = jnp.concatenate([p.astype(BF16) for p in probs], axis=0)
        if pending is not None:
            emit(*pending)
        pending = (i, p_all, denom)
    emit(*pending)


def _moba(proj, bsz, seq):
    n_blk = seq // MOBA_BLOCK
    assert seq % MOBA_BLOCK == 0 and n_blk <= SUBLANES
    dh = MOBA_HEAD_DIM
    kern = functools.partial(_moba_kernel, n_blk=n_blk, blk=MOBA_BLOCK, top_n=min(MOBA_TOPK, n_blk),
                             scale=dh ** -0.5)
    return pl.pallas_call(
        kern,
        grid=(bsz, MOBA_HEADS),
        in_specs=[pl.BlockSpec((seq, dh), lambda b, h: (b, OFF_AQ // dh + h)),
                  pl.BlockSpec((seq, dh), lambda b, h: (b, OFF_AK // dh + h)),
                  pl.BlockSpec((seq, dh), lambda b, h: (b, OFF_AV // dh + h))],
        out_specs=pl.BlockSpec((seq, dh), lambda b, h: (b, h)),
        out_shape=jax.ShapeDtypeStruct((bsz * seq, MOBA_WIDTH), BF16),
        scratch_shapes=[pltpu.VMEM((dh, seq), BF16)],
        compiler_params=_params("parallel", "parallel"),
        name="moba",
    )(proj, proj, proj)


def _ret_kernel(cd_ref, q_ref, k_ref, v_ref, g_ref, cos_ref, sin_ref, dec_ref, zeta_ref, xi_ref,
                o_ref, st_ref, *, dk, c):
    cd = cd_ref[pl.program_id(1)]
    n_c = q_ref.shape[0] // c
    half = dk // 2

    def widen(t):
        return jnp.concatenate([t] * (dk // LANES), axis=1)

    dec = dec_ref[...]
    xi = widen(xi_ref[...])
    zeta = widen(zeta_ref[...])

    def within_chunk(n):
        r = slice(n * c, (n + 1) * c)
        cos = cos_ref[r, :]
        sin = sin_ref[r, :]

        def rot(x):
            x1 = x[:, :half]
            x2 = x[:, half:]
            return jnp.concatenate([x1 * cos - x2 * sin, x1 * sin + x2 * cos], axis=1)

        q = rot(q_ref[r, :].astype(F32))
        k = rot(k_ref[r, :].astype(F32)) * (dk ** -0.5)
        v = v_ref[r, :]
        sc = lax.dot_general(q.astype(BF16), k.astype(BF16), _NT, preferred_element_type=F32) * dec
        inner = jnp.dot(sc.astype(BF16), v, preferred_element_type=F32)
        kv = jnp.dot((k * zeta).T.astype(BF16), v, preferred_element_type=F32) if n < n_c - 1 else None
        return inner, kv, (q * xi).astype(BF16)

    ahead = within_chunk(0)
    for n in range(n_c):
        r = slice(n * c, (n + 1) * c)
        o, kv, qx = ahead
        if n + 1 < n_c:
            ahead = within_chunk(n + 1)
        if n > 0:
            o = o + jnp.dot(qx, st_ref[...].astype(BF16), preferred_element_type=F32)
        if kv is not None:
            st_ref[...] = kv if n == 0 else cd * st_ref[...] + kv

        mu = jnp.mean(o, axis=1, keepdims=True)
        d = o - mu
        var = jnp.mean(d * d, axis=1, keepdims=True)
        gg = g_ref[r, :].astype(F32)
        half_g = 0.5 * gg
        o_ref[r, :] = ((half_g + half_g * jnp.tanh(half_g)) * (d * lax.rsqrt(var + GN_EPS))).astype(o_ref.dtype)


def _retention(proj, bsz, seq):
    c, dk, dv, nh = RET_CHUNK, RET_KEY_DIM, RET_VALUE_DIM, RET_HEADS
    n_c = seq // c
    half = dk // 2
    pos = jnp.arange(seq, dtype=F32)
    inv = ROPE_BASE ** (-jnp.linspace(0.0, 1.0, half, dtype=F32))
    ang = pos[:, None] * inv[None, :]
    cos, sin = jnp.cos(ang), jnp.sin(ang)
    log_g = jnp.log1p(-jnp.power(2.0, -5.0 - jnp.arange(nh, dtype=F32)))
    idx = jnp.arange(c, dtype=F32)
    diff = idx[:, None] - idx[None, :]
    decay = jnp.where(diff >= 0, jnp.exp(log_g[:, None, None] * jnp.maximum(diff, 0.0)), 0.0)
    zeta = jnp.exp(log_g[:, None] * (c - 1 - idx)[None, :])
    xi = jnp.exp(log_g[:, None] * (idx + 1.0)[None, :])
    zeta_b = jnp.broadcast_to(zeta[:, :, None], (nh, c, LANES))
    xi_b = jnp.broadcast_to(xi[:, :, None], (nh, c, LANES))
    chunk_decay = jnp.exp(log_g * c)

    return pl.pallas_call(
        functools.partial(_ret_kernel, dk=dk, c=c),
        grid=(bsz, nh),
        in_specs=[pl.BlockSpec(memory_space=pltpu.SMEM),
                  pl.BlockSpec((seq, dk), lambda b, h: (b, OFF_RQ // dk + h)),
                  pl.BlockSpec((seq, dk), lambda b, h: (b, OFF_RK // dk + h)),
                  pl.BlockSpec((seq, dv), lambda b, h: (b, OFF_RV // dv + h)),
                  pl.BlockSpec((seq, dv), lambda b, h: (b, OFF_RG // dv + h)),
                  pl.BlockSpec((seq, half), lambda b, h: (0, 0)),
                  pl.BlockSpec((seq, half), lambda b, h: (0, 0)),
                  pl.BlockSpec((None, c, c), lambda b, h: (h, 0, 0)),
                  pl.BlockSpec((None, c, LANES), lambda b, h: (h, 0, 0)),
                  pl.BlockSpec((None, c, LANES), lambda b, h: (h, 0, 0))],
        out_specs=pl.BlockSpec((seq, dv), lambda b, h: (b, h)),
        out_shape=jax.ShapeDtypeStruct((bsz * seq, RET_V_WIDTH), BF16),
        scratch_shapes=[pltpu.VMEM((dk, dv), F32)],
        compiler_params=_params("parallel", "parallel"),
        name="retention",
    )(chunk_decay, proj, proj, proj, proj, cos, sin, decay, zeta_b, xi_b)


def _mem_kernel(q_ref, k_ref, v_ref, o_ref, *, scale, sub):
    n_sub = q_ref.shape[0] // sub

    def scores(t):
        return lax.dot_general(q_ref[t * sub:(t + 1) * sub, :], k_ref[...], _NT, preferred_element_type=F32)

    s_next = scores(0)
    for t in range(n_sub):
        s = s_next * scale
        if t + 1 < n_sub:
            s_next = scores(t + 1)
        m = jnp.max(s, axis=1, keepdims=True)
        p = jnp.exp(s - m)
        p = p * (1.0 / jnp.sum(p, axis=1, keepdims=True))
        o_ref[t * sub:(t + 1) * sub, :] = jnp.dot(p.astype(BF16), v_ref[...],
                                                  preferred_element_type=F32).astype(o_ref.dtype)


def _mem_attention(proj, kvm, bsz, seq, mem_len):
    dh, nh = MEM_HEAD_DIM, MEM_HEADS
    tq = min(2048, seq)
    n_q = seq // tq
    return pl.pallas_call(
        functools.partial(_mem_kernel, scale=dh ** -0.5, sub=min(512, tq)),
        grid=(bsz, nh, n_q),
        in_specs=[pl.BlockSpec((tq, dh), lambda b, h, i: (b * n_q + i, OFF_MQ // dh + h)),
                  pl.BlockSpec((mem_len, dh), lambda b, h, i: (b, h)),
                  pl.BlockSpec((mem_len, dh), lambda b, h, i: (b, nh + h))],
        out_specs=pl.BlockSpec((tq, dh), lambda b, h, i: (b * n_q + i, h)),
        out_shape=jax.ShapeDtypeStruct((bsz * seq, MEM_WIDTH), BF16),
        compiler_params=_params("parallel", "parallel", "parallel"),
        name="mem_attention",
    )(proj, kvm, kvm)


def _merge_kernel(ya_ref, yr_ref, ym_ref, pa_ref, pr_ref, pm_ref, ga_ref, gr_ref, gm_ref, o_ref,
                  pab_ref, prb_ref, pmb_ref):
    @pl.when(pl.program_id(1) == 0)
    def _():
        pab_ref[...] = pa_ref[...].astype(BF16)
        prb_ref[...] = pr_ref[...].astype(BF16)
        pmb_ref[...] = pm_ref[...].astype(BF16)

    def branch(y_ref, p_ref, gate_ref):
        t = jnp.dot(y_ref[...], p_ref[...], preferred_element_type=F32)
        return jax.nn.sigmoid(gate_ref[...].astype(F32)) * t

    o_ref[...] = (branch(ya_ref, pab_ref, ga_ref) + branch(yr_ref, prb_ref, gr_ref)
                  + branch(ym_ref, pmb_ref, gm_ref)).astype(o_ref.dtype)


def _merge(y_a, y_r, y_m, p_moba, p_ret, p_mem, layer, proj):
    m = y_a.shape[0]
    tm, tn = min(512, m), 1024
    d = D_MODEL

    def rows(width):
        return pl.BlockSpec((tm, width), lambda j, i: (i, 0))

    def wcols(kdim):
        return pl.BlockSpec((None, kdim, tn), lambda j, i: (layer, 0, j), pipeline_mode=pl.Buffered(1))

    def gate(off):
        return pl.BlockSpec((tm, tn), lambda j, i: (i, off // tn + j))

    return pl.pallas_call(
        _merge_kernel,
        grid=(d // tn, m // tm),
        in_specs=[rows(MOBA_WIDTH), rows(RET_V_WIDTH), rows(MEM_WIDTH),
                  wcols(MOBA_WIDTH), wcols(RET_V_WIDTH), wcols(MEM_WIDTH),
                  gate(OFF_GA), gate(OFF_GR), gate(OFF_GM)],
        out_specs=pl.BlockSpec((tm, tn), lambda j, i: (i, j)),
        out_shape=jax.ShapeDtypeStruct((m, d), BF16),
        scratch_shapes=[pltpu.VMEM((MOBA_WIDTH, tn), BF16), pltpu.VMEM((RET_V_WIDTH, tn), BF16),
                        pltpu.VMEM((MEM_WIDTH, tn), BF16)],
        compiler_params=_params("parallel", "arbitrary"),
        name="merge",
    )(y_a, y_r, y_m, p_moba, p_ret, p_mem, proj, proj, proj)


def _layer_norm_rows(z, g, b):
    mu = jnp.mean(z, axis=1, keepdims=True)
    d = z - mu
    var = jnp.mean(d * d, axis=1, keepdims=True)
    return d * lax.rsqrt(var + LN_EPS) * g + b


def _wo_ln_router_kernel(mg_ref, wo_hbm, x_ref, g_ref, b_ref, wr2_ref, wr1_ref, br_ref,
                         x1_ref, ids_ref, wts_ref, wob_ref, stage_ref, sem, *, layer, sub):
    @pl.when(pl.program_id(0) == 0)
    def _():
        chunk = stage_ref.shape[0]
        for c in range(wob_ref.shape[0] // chunk):
            cp = pltpu.make_async_copy(wo_hbm.at[layer, pl.ds(c * chunk, chunk)], stage_ref, sem)
            cp.start()
            cp.wait()
            wob_ref[c * chunk:(c + 1) * chunk, :] = stage_ref[...].astype(BF16)

    n_sub = mg_ref.shape[0] // sub

    def mix_of(h):
        return jnp.dot(mg_ref[h * sub:(h + 1) * sub, :], wob_ref[...], preferred_element_type=F32)

    mix_next = mix_of(0)
    for h in range(n_sub):
        rows = slice(h * sub, (h + 1) * sub)
        mix = mix_next
        if h + 1 < n_sub:
            mix_next = mix_of(h + 1)
        _ln_router_rows(mix, x_ref[rows, :], g_ref, b_ref, wr2_ref, wr1_ref, br_ref,
                        x1_ref.at[rows, :], ids_ref.at[:, rows], wts_ref.at[:, rows])


def _ln_router_rows(mix, x, g_ref, b_ref, wr2_ref, wr1_ref, br_ref, x1_ref, ids_ref, wts_ref):
    x1 = _layer_norm_rows(DEEPNORM_ALPHA * x + mix, g_ref[...], b_ref[...])
    x1_ref[...] = x1

    r = ROUTER_ROWS
    x_hi = x1.astype(BF16)
    x_lo = (x1 - x_hi.astype(F32)).astype(BF16)
    l2 = lax.dot_general(wr2_ref[...], x_hi, _NT, preferred_element_type=F32)
    l1 = lax.dot_general(wr1_ref[...], x_lo, _NT, preferred_element_type=F32)
    logit = l2[0:r] + l2[r:2 * r] + l1 + br_ref[...]

    tok = logit.shape[1]
    row = lax.broadcasted_iota(jnp.int32, (SUBLANES, tok), 0).astype(F32)
    gl = jnp.where(row < N_GROUPS, logit[0:SUBLANES], NEG_INF)
    g_max = jnp.max(gl, axis=0, keepdims=True)
    g_sel = jnp.min(jnp.where(gl == g_max, row, float(SUBLANES)), axis=0, keepdims=True)
    p_g = 1.0 / jnp.sum(jnp.exp(gl - g_max), axis=0, keepdims=True)
    e_in = jnp.zeros((SUBLANES, tok), F32)
    for grp in range(N_GROUPS):
        lo = SUBLANES * (1 + grp)
        e_in = jnp.where(g_sel == float(grp), logit[lo:lo + EXPERTS_PER_GROUP], e_in)
    v1 = jnp.max(e_in, axis=0, keepdims=True)
    i1 = jnp.min(jnp.where(e_in == v1, row, float(SUBLANES)), axis=0, keepdims=True)
    e_rest = jnp.where(row == i1, -jnp.inf, e_in)
    v2 = jnp.max(e_rest, axis=0, keepdims=True)
    i2 = jnp.min(jnp.where(e_rest == v2, row, float(SUBLANES)), axis=0, keepdims=True)
    t = jnp.exp(v2 - v1)
    w1 = p_g / (1.0 + t)
    w2 = w1 * t
    e1 = g_sel * float(EXPERTS_PER_GROUP) + i1
    e2 = g_sel * float(EXPERTS_PER_GROUP) + i2
    ids_ref[...] = jnp.where(row == 0.0, e1, jnp.where(row == 1.0, e2, 0.0)).astype(jnp.int32)
    wts_ref[...] = jnp.where(row == 0.0, w1, jnp.where(row == 1.0, w2, 0.0))


def _wo_ln_router(merged, w_o, layer, x, ln_g, ln_b, w_group, b_group, w_expert, b_expert):
    m, d = x.shape
    sub = min(ROW_TILE, m)
    tm = min(2 * ROW_TILE, m)
    r = ROUTER_ROWS
    wr = jnp.zeros((r, d), F32).at[0:N_GROUPS].set(w_group.T).at[SUBLANES:SUBLANES + N_EXPERTS].set(w_expert.T)
    br = jnp.zeros((r, 1), F32).at[0:N_GROUPS, 0].set(b_group).at[SUBLANES:SUBLANES + N_EXPERTS, 0].set(b_expert)
    wr_hi = wr.astype(BF16)
    wr_lo = (wr - wr_hi.astype(F32)).astype(BF16)
    wr2 = jnp.concatenate([wr_hi, wr_lo], axis=0)

    def whole(shape):
        return pl.BlockSpec(shape, lambda i: (0,) * len(shape))

    return pl.pallas_call(
        functools.partial(_wo_ln_router_kernel, layer=layer, sub=sub),
        grid=(m // tm,),
        in_specs=[pl.BlockSpec((tm, d), lambda i: (i, 0)),
                  pl.BlockSpec(memory_space=pl.ANY),
                  pl.BlockSpec((tm, d), lambda i: (i, 0)),
                  whole((1, d)), whole((1, d)),
                  whole((2 * r, d)), whole((r, d)), whole((r, 1))],
        out_specs=[pl.BlockSpec((tm, d), lambda i: (i, 0)),
                   pl.BlockSpec((SUBLANES, tm), lambda i: (0, i)),
                   pl.BlockSpec((SUBLANES, tm), lambda i: (0, i))],
        out_shape=[jax.ShapeDtypeStruct((m, d), F32),
                   jax.ShapeDtypeStruct((SUBLANES, m), jnp.int32),
                   jax.ShapeDtypeStruct((SUBLANES, m), F32)],
        scratch_shapes=[pltpu.VMEM((d, d), BF16), pltpu.VMEM((d // 4, d), F32), pltpu.SemaphoreType.DMA(())],
        compiler_params=_params("arbitrary"),
        name="wo_ln_router",
    )(merged, w_o, x, ln_g.reshape(1, d), ln_b.reshape(1, d), wr2, wr_hi, br)


def _row_copy(src, src_row, dst, dst_row, sem):
    return pltpu.make_async_copy(src.at[pl.ds(src_row, 1)], dst.at[pl.ds(dst_row, 1)], sem)


def _rank_kernel(e_ref, pos_ref, cnt_ref, tri_ref, carry_ref):
    n = e_ref.shape[1]

    @pl.when(pl.program_id(0) == 0)
    def _():
        r = lax.broadcasted_iota(jnp.int32, (n, n), 0)
        c = lax.broadcasted_iota(jnp.int32, (n, n), 1)
        tri_ref[...] = jnp.where(r <= c, 1.0, 0.0).astype(BF16)
        carry_ref[...] = jnp.zeros_like(carry_ref)

    row = lax.broadcasted_iota(jnp.int32, (N_EXPERTS, n), 0)
    onehot = jnp.where(row == e_ref[...], 1.0, 0.0)
    prefix = jnp.dot(onehot.astype(BF16), tri_ref[...], preferred_element_type=F32)
    carry = carry_ref[...]
    pos = jnp.sum(onehot * (prefix + carry[:, 0:1]), axis=0, keepdims=True) - 1.0
    pos_ref[...] = pos.astype(jnp.int32)
    carry = carry + jnp.sum(onehot, axis=1, keepdims=True)
    carry_ref[...] = carry
    cnt_ref[...] = carry


def _rank(flat_e):
    n_asg = flat_e.shape[1]
    tile = min(2048, n_asg)
    return pl.pallas_call(
        _rank_kernel,
        grid=(n_asg // tile,),
        in_specs=[pl.BlockSpec((1, tile), lambda i: (0, i))],
        out_specs=[pl.BlockSpec((1, tile), lambda i: (0, i)),
                   pl.BlockSpec((N_EXPERTS, LANES), lambda i: (0, 0))],
        out_shape=[jax.ShapeDtypeStruct((1, n_asg), jnp.int32),
                   jax.ShapeDtypeStruct((N_EXPERTS, LANES), F32)],
        scratch_shapes=[pltpu.VMEM((tile, tile), BF16), pltpu.VMEM((N_EXPERTS, LANES), F32)],
        compiler_params=_params("arbitrary"),
        name="rank",
    )(flat_e)


def _invert_kernel(d0_ref, d1_ref, fill_hbm, o_ref, sem):
    fill = pltpu.make_async_copy(fill_hbm, o_ref, sem)
    fill.start()
    fill.wait()

    def put(t, c):
        o_ref[d0_ref[t]] = t
        o_ref[d1_ref[t]] = t
        return c

    lax.fori_loop(0, d0_ref.shape[0], put, 0, unroll=8)


def _invert(dest0, dest1, n_pad):
    smem = pl.BlockSpec(memory_space=pltpu.SMEM)
    return pl.pallas_call(
        _invert_kernel,
        in_specs=[smem, smem, pl.BlockSpec(memory_space=pl.ANY)],
        out_specs=smem,
        out_shape=jax.ShapeDtypeStruct((n_pad,), jnp.int32),
        scratch_shapes=[pltpu.SemaphoreType.DMA(())],
        name="invert",
    )(dest0, dest1, jnp.arange(n_pad, dtype=jnp.int32) % dest0.shape[0])


GATHER_SLOTS = 3


def _expert_kernel(be_ref, nx_ref, nu_ref, rt_ref, x_hbm, wgu_hbm, wd_hbm, o_ref,
                   xbuf, wgu_s, wd_s, wgu_b, wd_b, gsem, wsem, *, rows, layer):
    i = pl.program_id(0)
    nu = nu_ref[0]
    ff = wd_b.shape[0]
    ns = GATHER_SLOTS

    def issue(blk, slot):
        @pl.when(blk < nu)
        def _():
            base = blk * rows
            for r in range(rows):
                _row_copy(x_hbm, rt_ref[base + r], xbuf.at[slot], r, gsem.at[slot]).start()

    def drain(slot):
        for _ in range(rows):
            _row_copy(x_hbm, 0, xbuf.at[slot], 0, gsem.at[slot]).wait()

    def weight_copies(e):
        return (pltpu.make_async_copy(wgu_hbm.at[layer, e], wgu_s, wsem.at[0]),
                pltpu.make_async_copy(wd_hbm.at[layer, e], wd_s, wsem.at[1]))

    @pl.when(i == 0)
    def _():
        for cp in weight_copies(be_ref[0]):
            cp.start()
        for b in range(ns - 1):
            issue(b, b)

    @pl.when((i < nu) & ((i == 0) | (be_ref[i] != be_ref[jnp.maximum(i - 1, 0)])))
    def _():
        for cp in weight_copies(be_ref[i]):
            cp.wait()
        wgu_b[...] = wgu_s[...].astype(BF16)
        wd_b[...] = wd_s[...].astype(BF16)

        @pl.when(nx_ref[i] >= 0)
        def _():
            for cp in weight_copies(nx_ref[i]):
                cp.start()

    def compute(slot):
        issue(i + ns - 1, (slot + ns - 1) % ns)
        drain(slot)
        gu = jnp.dot(xbuf[slot].astype(BF16), wgu_b[...], preferred_element_type=F32)
        gate = gu[:, :ff]
        hmid = gate * jax.nn.sigmoid(gate) * gu[:, ff:]
        o_ref[...] = jnp.dot(hmid.astype(BF16), wd_b[...], preferred_element_type=F32)

    phase = lax.rem(i, ns)
    for slot in range(ns):
        @pl.when((i < nu) & (phase == slot))
        def _(slot=slot):
            compute(slot)

    @pl.when(i >= nu)
    def _():
        o_ref[...] = jnp.zeros_like(o_ref)


def _experts(x1, row_tok, block_e, next_e, n_used, w_gate_up, w_down, layer):
    d = x1.shape[1]
    n_pad = row_tok.shape[0]
    ff = w_down.shape[2]
    rows = EXPERT_ROWS
    return pl.pallas_call(
        functools.partial(_expert_kernel, rows=rows, layer=layer),
        grid_spec=pltpu.PrefetchScalarGridSpec(
            num_scalar_prefetch=4,
            grid=(n_pad // rows,),
            in_specs=[pl.BlockSpec(memory_space=pl.ANY),
                      pl.BlockSpec(memory_space=pl.ANY),
                      pl.BlockSpec(memory_space=pl.ANY)],
            out_specs=pl.BlockSpec((rows, d), lambda i, be, nx, nu, rt: (i, 0)),
            scratch_shapes=[pltpu.VMEM((GATHER_SLOTS, rows, d), F32),
                            pltpu.VMEM((d, 2 * ff), F32),
                            pltpu.VMEM((ff, d), F32),
                            pltpu.VMEM((d, 2 * ff), BF16),
                            pltpu.VMEM((ff, d), BF16),
                            pltpu.SemaphoreType.DMA((GATHER_SLOTS,)),
                            pltpu.SemaphoreType.DMA((2,))]),
        out_shape=jax.ShapeDtypeStruct((n_pad, d), F32),
        compiler_params=_params("arbitrary"),
        name="experts",
    )(block_e, next_e, n_used, row_tok, x1, w_gate_up, w_down)


COMBINE_SLOTS = 3


def _combine_ln_kernel(d0_ref, d1_ref, y_hbm, w_ref, x_ref, g_ref, b_ref, o_ref, ob_ref, buf, sem,
                       *, tb, n_steps):
    i = pl.program_id(0)

    def issue(step, slot):
        base = step * tb
        for t in range(tb):
            _row_copy(y_hbm, d0_ref[base + t], buf.at[slot, 0], t, sem.at[slot]).start()
            _row_copy(y_hbm, d1_ref[base + t], buf.at[slot, 1], t, sem.at[slot]).start()

    def drain(slot):
        for _ in range(EXPERT_TOPK * tb):
            _row_copy(y_hbm, 0, buf.at[slot, 0], 0, sem.at[slot]).wait()

    ns = COMBINE_SLOTS

    @pl.when(i == 0)
    def _():
        for s in range(min(ns - 1, n_steps)):
            issue(s, s)

    def compute(slot):
        drain(slot)
        w = w_ref[...]
        ffn = w[:, 0:1] * buf[slot, 0] + w[:, 1:2] * buf[slot, 1]
        out = _layer_norm_rows(DEEPNORM_ALPHA * x_ref[...] + ffn, g_ref[...], b_ref[...])
        o_ref[...] = out
        ob_ref[...] = out.astype(BF16)

    phase = lax.rem(i, ns)
    for slot in range(ns):
        @pl.when((phase == slot) & (i + ns - 1 < n_steps))
        def _(slot=slot):
            issue(i + ns - 1, (slot + ns - 1) % ns)

        @pl.when(phase == slot)
        def _(slot=slot):
            compute(slot)


def _combine_ln(y_rows, dest0, dest1, w_tok, x1, ln_g, ln_b):
    m, d = x1.shape
    tb = min(256, m)
    n_steps = m // tb
    return pl.pallas_call(
        functools.partial(_combine_ln_kernel, tb=tb, n_steps=n_steps),
        grid_spec=pltpu.PrefetchScalarGridSpec(
            num_scalar_prefetch=2,
            grid=(n_steps,),
            in_specs=[pl.BlockSpec(memory_space=pl.ANY),
                      pl.BlockSpec((tb, EXPERT_TOPK), lambda i, a, b: (i, 0)),
                      pl.BlockSpec((tb, d), lambda i, a, b: (i, 0)),
                      pl.BlockSpec((1, d), lambda i, a, b: (0, 0)),
                      pl.BlockSpec((1, d), lambda i, a, b: (0, 0))],
            out_specs=[pl.BlockSpec((tb, d), lambda i, a, b: (i, 0)),
                       pl.BlockSpec((tb, d), lambda i, a, b: (i, 0))],
            scratch_shapes=[pltpu.VMEM((COMBINE_SLOTS, EXPERT_TOPK, tb, d), F32),
                            pltpu.SemaphoreType.DMA((COMBINE_SLOTS,))]),
        out_shape=[jax.ShapeDtypeStruct((m, d), F32), jax.ShapeDtypeStruct((m, d), BF16)],
        compiler_params=_params("arbitrary"),
        name="combine_ln",
    )(dest0, dest1, y_rows, w_tok, x1, ln_g.reshape(1, d), ln_b.reshape(1, d))


def _slot_plan(ids, n_tok):
    rows = EXPERT_ROWS
    n_asg = n_tok * EXPERT_TOPK
    n_pad = n_asg + N_EXPERTS * rows
    flat_e = ids[:EXPERT_TOPK].reshape(1, n_asg)
    pos, cnt = _rank(flat_e)
    counts = cnt[:, 0].astype(jnp.int32)
    padded = ((counts + rows - 1) // rows) * rows
    pad_ends = jnp.cumsum(padded)
    pad_starts = pad_ends - padded
    experts = jnp.arange(N_EXPERTS, dtype=jnp.int32)
    start_of = jnp.sum(jnp.where(flat_e[0][:, None] == experts[None, :], pad_starts[None, :], 0), axis=1)
    dest = (start_of + pos[0]).astype(jnp.int32)
    n_blocks = n_pad // rows
    block_start = jnp.arange(n_blocks, dtype=jnp.int32) * rows
    block_e = jnp.minimum(jnp.sum((pad_ends[None, :] <= block_start[:, None]).astype(jnp.int32), axis=1),
                          N_EXPERTS - 1)
    later = (experts[None, :] > experts[:, None]) & (counts[None, :] > 0)
    next_of = jnp.min(jnp.where(later, experts[None, :], N_EXPERTS), axis=1)
    next_of = jnp.where(next_of == N_EXPERTS, -1, next_of)
    next_e = jnp.sum(jnp.where(block_e[:, None] == experts[None, :], next_of[None, :], 0), axis=1).astype(jnp.int32)
    n_used = (pad_ends[-1:] // rows).astype(jnp.int32)
    dest0, dest1 = dest[:n_tok], dest[n_tok:]
    return dest0, dest1, _invert(dest0, dest1, n_pad), block_e, next_e, n_used


def kernel(x, mem, w_in, p_moba, p_ret, p_mem, w_mem_kv, w_o, ln1_g, ln1_b, w_group, b_group,
           w_expert, b_expert, w_gate_up, w_down, ln2_g, ln2_b):
    bsz, seq, d = x.shape
    mem_len = mem.shape[1]
    n_tok = bsz * seq
    xf = x.reshape(n_tok, d)
    xb = xf
    mem_f = mem.reshape(bsz * mem_len, d)
    for l in range(w_in.shape[0]):
        proj = _matmul(xb, w_in, l, BF16, 1024, 1024)
        kvm = _matmul(mem_f, w_mem_kv, l, BF16, 512, 1024)
        y_a = _moba(proj, bsz, seq)
        y_r = _retention(proj, bsz, seq)
        y_m = _mem_attention(proj, kvm, bsz, seq, mem_len)
        merged = _merge(y_a, y_r, y_m, p_moba, p_ret, p_mem, l, proj)
        x1, ids, wts = _wo_ln_router(merged, w_o, l, xf, ln1_g[l], ln1_b[l],
                                     w_group[l], b_group[l], w_expert[l], b_expert[l])
        dest0, dest1, row_tok, block_e, next_e, n_used = _slot_plan(ids, n_tok)
        y_rows = _experts(x1, row_tok, block_e, next_e, n_used, w_gate_up, w_down, l)
        xf, xb = _combine_ln(y_rows, dest0, dest1, wts[:EXPERT_TOPK].T, x1, ln2_g[l], ln2_b[l])
    return xf.reshape(bsz, seq, d)
```
